```python
import math
import jax, jax.numpy as jnp
from jax import lax
import numpy as np

D_MODEL = 1024
BATCH = 32
SEQ = 2048
DEPTH = 4

N_MIXERS = 3
D_FF = 4 * D_MODEL
D_RNN = D_MODEL
HA = 8
HDA = D_RNN // HA
CONV_W = 4
RG_C = 8.0
SGU_CHUNK = 128
D_SGU = D_MODEL
GB = 8
DGB = D_SGU // GB
HC = 8
DK = 128
DV = 128
DC = HC * DV
GDN_CHUNK = 64

kernel_name = "hybrid_rglru_sgu_gdn_encoder"

F32 = jnp.float32


def rms_norm(x, g, eps=1e-6):
    xf = x.astype(F32)
    y = xf * lax.rsqrt(jnp.mean(xf * xf, axis=-1, keepdims=True) + eps)
    return (y * g.astype(F32)).astype(x.dtype)


def layer_norm(x, g, b, eps=1e-5):
    xf = x.astype(F32)
    mu = jnp.mean(xf, axis=-1, keepdims=True)
    var = jnp.mean(jnp.square(xf - mu), axis=-1, keepdims=True)
    return ((xf - mu) * lax.rsqrt(var + eps) * g.astype(F32) + b.astype(F32)).astype(x.dtype)


def l2_normalize(x, eps=1e-6):
    xf = x.astype(F32)
    return xf * lax.rsqrt(jnp.sum(xf * xf, axis=-1, keepdims=True) + eps)


def centred_dwconv(x, w):
    K, C = w.shape
    left = K // 2
    return lax.conv_general_dilated(x, w[:, None, :].astype(x.dtype), window_strides=(1,),
                                    padding=[(left, K - 1 - left)],
                                    dimension_numbers=('NWC', 'WIO', 'NWC'),
                                    feature_group_count=C)


def linear_scan(a, b, reverse):
    def combine(c1, c2):
        a1, b1 = c1
        a2, b2 = c2
        return a1 * a2, a2 * b1 + b2
    _, h = lax.associative_scan(combine, (a, b), axis=1, reverse=reverse)
    return h


def rglru_direction(xr, gate_w, gate_b, lam, reverse):
    Bsz, S, _ = xr.shape
    xh = xr.reshape(Bsz, S, HA, HDA)
    gates = jnp.einsum('bshi,ghij->gbshj', xh, gate_w) + gate_b[:, None, None]
    gates = jax.nn.sigmoid(gates.astype(F32)).reshape(2, Bsz, S, D_RNN)
    r, ig = gates[0], gates[1]
    log_a = -RG_C * r * jax.nn.softplus(-lam.astype(F32))
    a = jnp.exp(log_a)
    mult = jnp.sqrt(-jnp.expm1(2.0 * log_a))
    bx = mult * ig * xr.astype(F32)
    return linear_scan(a, bx, reverse)


def mixer_rglru(h, w_in, conv_w, conv_b, gate_w, gate_b, lam, w_out):
    z = h @ w_in
    gate, xr = z[..., :D_RNN], z[..., D_RNN:]
    xr = centred_dwconv(xr, conv_w) + conv_b
    y = (rglru_direction(xr, gate_w[0], gate_b[0], lam[0], reverse=False)
         + rglru_direction(xr, gate_w[1], gate_b[1], lam[1], reverse=True))
    y = y.astype(h.dtype) * jax.nn.gelu(gate)
    return y @ w_out


def mixer_sgu(h, w_in, ln_g, ln_b, w_s, b_s, w_out):
    Bsz, S, _ = h.shape
    n_chunks = S // SGU_CHUNK
    z = jax.nn.gelu(h @ w_in)
    u, v = z[..., :D_SGU], z[..., D_SGU:]
    v = layer_norm(v, ln_g, ln_b).reshape(Bsz, n_chunks, SGU_CHUNK, GB, DGB)
    vs = jnp.einsum('gpq,bnqgc->bnpgc', w_s, v) + b_s.T[None, None, :, :, None]
    y = u * vs.reshape(Bsz, S, D_SGU)
    return y @ w_out


def chunk_gated_delta(q, k, v, g, beta):
    Bsz, S, H, _ = q.shape
    C = GDN_CHUNK
    N = S // C
    ch = lambda t: t.reshape(Bsz, N, C, H, t.shape[-1]).transpose(0, 3, 1, 2, 4).astype(F32)
    chs = lambda t: t.reshape(Bsz, N, C, H).transpose(0, 3, 1, 2).astype(F32)
    q, k, v = ch(q), ch(k), ch(v)
    g, beta = chs(g), chs(beta)
    gc = jnp.cumsum(g, axis=-1)
    tril = jnp.tril(jnp.ones((C, C), bool))
    strict = jnp.tril(jnp.ones((C, C), bool), -1)
    diff = gc[..., :, None] - gc[..., None, :]
    decay = jnp.where(tril, jnp.exp(jnp.where(tril, diff, 0.0)), 0.0)
    k_beta = k * beta[..., None]
    v_beta = v * beta[..., None]
    A = jnp.where(strict, jnp.einsum('bhnid,bhnjd->bhnij', k_beta, k) * decay, 0.0)
    eye = jnp.eye(C, dtype=F32)
    T = lax.linalg.triangular_solve(eye + A, jnp.broadcast_to(eye, A.shape), left_side=True,
                                    lower=True, unit_diagonal=True)
    u = jnp.einsum('bhnij,bhnjd->bhnid', T, v_beta)
    w = jnp.einsum('bhnij,bhnjd->bhnid', T, k_beta * jnp.exp(gc)[..., None])
    qk = jnp.einsum('bhnid,bhnjd->bhnij', q, k) * decay

    def step(state, xs):
        q_c, k_c, u_c, w_c, qk_c, gc_c = xs
        v_new = u_c - jnp.einsum('bhck,bhkv->bhcv', w_c, state)
        o = (jnp.einsum('bhck,bhkv->bhcv', q_c * jnp.exp(gc_c)[..., None], state)
             + jnp.einsum('bhij,bhjv->bhiv', qk_c, v_new))
        g_last = gc_c[..., -1]
        state = (state * jnp.exp(g_last)[..., None, None]
                 + jnp.einsum('bhck,bhcv->bhkv', k_c * jnp.exp(g_last[..., None] - gc_c)[..., None], v_new))
        return state, o

    mv = lambda t: jnp.moveaxis(t, 2, 0)
    state0 = jnp.zeros((Bsz, H, q.shape[-1], v.shape[-1]), F32)
    _, o = lax.scan(step, state0, (mv(q), mv(k), mv(u), mv(w), mv(qk), mv(gc)))
    return o.transpose(1, 0, 3, 2, 4).reshape(Bsz, S, H, v.shape[-1])


def mixer_gdn(h, w_in, conv_w, a_log, dt_bias, norm_g, w_out):
    Bsz, S, _ = h.shape
    z = h @ w_in
    qkv = jax.nn.silu(centred_dwconv(z[..., :3 * DC], conv_w))
    gate = z[..., 3 * DC:4 * DC].reshape(Bsz, S, HC, DV)
    a_logit = z[..., 4 * DC:4 * DC + 2 * HC].reshape(Bsz, S, 2, HC).astype(F32)
    b_logit = z[..., 4 * DC + 2 * HC:].reshape(Bsz, S, 2, HC).astype(F32)
    q = l2_normalize(qkv[..., :DC].reshape(Bsz, S, HC, DK)) * (DK ** -0.5)
    k = l2_normalize(qkv[..., DC:2 * DC].reshape(Bsz, S, HC, DK))
    v = qkv[..., 2 * DC:].reshape(Bsz, S, HC, DV)
    g = -jnp.exp(a_log.astype(F32)) * jax.nn.softplus(a_logit + dt_bias.astype(F32))
    beta = jax.nn.sigmoid(b_logit)
    o_f = chunk_gated_delta(q, k, v, g[:, :, 0], beta[:, :, 0])
    fl = lambda t: jnp.flip(t, axis=1)
    o_b = fl(chunk_gated_delta(fl(q), fl(k), fl(v), fl(g[:, :, 1]), fl(beta[:, :, 1])))
    o = (o_f + o_b).astype(h.dtype)
    o = rms_norm(o, norm_g) * jax.nn.silu(gate)
    return o.reshape(Bsz, S, DC) @ w_out


def sqrelu_mlp(h, w_up, w_down):
    return jnp.square(jax.nn.relu(h @ w_up)) @ w_down


def setup_inputs(seed: int = 0) -> dict:
    key = jax.random.key(seed)
    ks = jax.random.split(key, 32)
    nA = (DEPTH + 2) // 3
    nB = (DEPTH + 1) // 3
    nC = DEPTH // 3
    nrm = lambda k, shape, scale: jax.random.normal(k, shape, F32) * scale
    gain = lambda k, shape: 1.0 + 0.02 * jax.random.normal(k, shape, F32)

    x = nrm(ks[0], (BATCH, SEQ, D_MODEL), 1.0)
    norm_mix_g = gain(ks[1], (DEPTH, D_MODEL))
    norm_mlp_g = gain(ks[2], (DEPTH, D_MODEL))
    mlp_w_up = nrm(ks[3], (DEPTH, D_MODEL, D_FF), D_MODEL ** -0.5)
    mlp_w_down = nrm(ks[4], (DEPTH, D_FF, D_MODEL), D_FF ** -0.5)
    norm_final_g = gain(ks[5], (D_MODEL,))

    a_w_in = nrm(ks[6], (nA, D_MODEL, 2 * D_RNN), D_MODEL ** -0.5)
    a_conv_w = nrm(ks[7], (nA, CONV_W, D_RNN), CONV_W ** -0.5)
    a_conv_b = nrm(ks[8], (nA, D_RNN), 0.02)
    a_gate_w = nrm(ks[9], (nA, 2, 2, HA, HDA, HDA), HDA ** -0.5)
    a_gate_b = nrm(ks[10], (nA, 2, 2, HA, HDA), 0.02)
    a0 = jax.random.uniform(ks[11], (nA, 2, D_RNN), F32, minval=0.9, maxval=0.999)
    a_base = a0 ** (1.0 / RG_C)
    a_lambda = jnp.log(a_base) - jnp.log1p(-a_base)
    a_w_out = nrm(ks[12], (nA, D_RNN, D_MODEL), D_RNN ** -0.5)

    b_w_in = nrm(ks[13], (nB, D_MODEL, 2 * D_SGU), D_MODEL ** -0.5)
    b_ln_g = gain(ks[14], (nB, D_SGU))
    b_ln_b = nrm(ks[15], (nB, D_SGU), 0.02)
    b_w_s = nrm(ks[16], (nB, GB, SGU_CHUNK, SGU_CHUNK), SGU_CHUNK ** -0.5)
    b_b_s = gain(ks[17], (nB, GB, SGU_CHUNK))
    b_w_out = nrm(ks[18], (nB, D_SGU, D_MODEL), D_SGU ** -0.5)

    c_w_in = nrm(ks[19], (nC, D_MODEL, 4 * DC + 4 * HC), D_MODEL ** -0.5)
    c_conv_w = nrm(ks[20], (nC, CONV_W, 3 * DC), CONV_W ** -0.5)
    c_a_log = jnp.log(jax.random.uniform(ks[21], (nC, 2, HC), F32, minval=1.0, maxval=16.0))
    dt = jnp.exp(jax.random.uniform(ks[22], (nC, 2, HC), F32, minval=math.log(1e-3), maxval=math.log(1e-1)))
    c_dt_bias = dt + jnp.log(-jnp.expm1(-dt))
    c_norm_g = gain(ks[23], (nC, DV))
    c_w_out = nrm(ks[24], (nC, DC, D_MODEL), DC ** -0.5)

    return {"x": x, "norm_mix_g": norm_mix_g, "norm_mlp_g": norm_mlp_g,
            "mlp_w_up": mlp_w_up, "mlp_w_down": mlp_w_down, "norm_final_g": norm_final_g,
            "a_w_in": a_w_in, "a_conv_w": a_conv_w, "a_conv_b": a_conv_b,
            "a_gate_w": a_gate_w, "a_gate_b": a_gate_b, "a_lambda": a_lambda, "a_w_out": a_w_out,
            "b_w_in": b_w_in, "b_ln_g": b_ln_g, "b_ln_b": b_ln_b, "b_w_s": b_w_s,
            "b_b_s": b_b_s, "b_w_out": b_w_out,
            "c_w_in": c_w_in, "c_conv_w": c_conv_w, "c_a_log": c_a_log, "c_dt_bias": c_dt_bias,
            "c_norm_g": c_norm_g, "c_w_out": c_w_out}


def reference(x, norm_mix_g, norm_mlp_g, mlp_w_up, mlp_w_down, norm_final_g,
              a_w_in, a_conv_w, a_conv_b, a_gate_w, a_gate_b, a_lambda, a_w_out,
              b_w_in, b_ln_g, b_ln_b, b_w_s, b_b_s, b_w_out,
              c_w_in, c_conv_w, c_a_log, c_dt_bias, c_norm_g, c_w_out):
    for i in range(DEPTH):
        kind, j = i % N_MIXERS, i // N_MIXERS
        hn = rms_norm(x, norm_mix_g[i])
        if kind == 0:
            m = mixer_rglru(hn, a_w_in[j], a_conv_w[j], a_conv_b[j], a_gate_w[j], a_gate_b[j],
                            a_lambda[j], a_w_out[j])
        elif kind == 1:
            m = mixer_sgu(hn, b_w_in[j], b_ln_g[j], b_ln_b[j], b_w_s[j], b_b_s[j], b_w_out[j])
        else:
            m = mixer_gdn(hn, c_w_in[j], c_conv_w[j], c_a_log[j], c_dt_bias[j], c_norm_g[j], c_w_out[j])
        x = x + m
        x = x + sqrelu_mlp(rms_norm(x, norm_mlp_g[i]), mlp_w_up[i], mlp_w_down[i])
    return rms_norm(x, norm_final_g)
```

```python
import functools

import jax
import jax.numpy as jnp
from jax import lax
from jax.experimental import pallas as pl
from jax.experimental.pallas import tpu as pltpu

F32 = jnp.float32
BF16 = jnp.bfloat16

D_MODEL = 1024
D_FF = 4 * D_MODEL
N_MIXERS = 3
LANES = 128
SUBLANES = 8
VMEM_LIMIT = 56 * 1024 * 1024

HA = 8
HDA = D_MODEL // HA
CONV_W = 4
CONV_LEFT = CONV_W // 2
RG_C = 8.0
SEG_PAD = 8
SGU_CHUNK = 128
GB = 8
DGB = D_MODEL // GB
HC = 8
DK = 128
DV = 128
DC = HC * DV
GDN_CHUNK = 64
GDN_BLOCK = 2 * GDN_CHUNK


def _params(n_parallel):
    return pltpu.CompilerParams(dimension_semantics=("parallel",) * n_parallel,
                                vmem_limit_bytes=VMEM_LIMIT)


def _rms(xf, g, eps=1e-6):
    return xf * lax.rsqrt(jnp.mean(xf * xf, axis=-1, keepdims=True) + eps) * g


def _sigmoid(x):
    return 1.0 / (1.0 + jnp.exp(-x))


def _silu(x):
    return x * _sigmoid(x)


def _gelu_tanh(x):
    c = 0.7978845608028654
    return 0.5 * x * (1.0 + jnp.tanh(c * (x + 0.044715 * (x * x * x))))


def _softplus(x):
    return jnp.maximum(x, 0.0) + jnp.log1p(jnp.exp(-jnp.abs(x)))


def _dot(a, b):
    return jnp.dot(a, b, preferred_element_type=F32)


def _dot_nt(a, b):
    return lax.dot_general(a, b, (((1,), (1,)), ((), ())), preferred_element_type=F32)


MLP_TM = 512
MLP_FC = 1024


def _mlp_kernel(x_ref, g_ref, wup_ref, wdn_ref, *rest, final_norm):
    if final_norm:
        gf_ref, o_ref = rest
    else:
        (o_ref,) = rest
    x = x_ref[...]
    h = _rms(x, g_ref[...]).astype(BF16)
    acc = x
    for c in range(D_FF // MLP_FC):
        u = _dot(h, wup_ref[:, c * MLP_FC:(c + 1) * MLP_FC])
        a = jnp.square(jnp.maximum(u, 0.0)).astype(BF16)
        acc = acc + _dot(a, wdn_ref[c * MLP_FC:(c + 1) * MLP_FC, :])
    if final_norm:
        acc = _rms(acc, gf_ref[...])
    o_ref[...] = acc


def _const_spec(shape):
    n = len(shape)
    return pl.BlockSpec(shape, lambda *_: (0,) * n, pipeline_mode=pl.Buffered(1))


def _mlp(x2, g, w_up, w_down, g_final=None):
    T = x2.shape[0]
    tok = pl.BlockSpec((MLP_TM, D_MODEL), lambda i: (i, 0))
    in_specs = [tok, _const_spec((1, D_MODEL)), _const_spec((D_MODEL, D_FF)), _const_spec((D_FF, D_MODEL))]
    args = [x2, g.reshape(1, D_MODEL), w_up, w_down]
    if g_final is not None:
        in_specs.append(_const_spec((1, D_MODEL)))
        args.append(g_final.reshape(1, D_MODEL))
    return pl.pallas_call(
        functools.partial(_mlp_kernel, final_norm=g_final is not None),
        grid=(T // MLP_TM,),
        in_specs=in_specs,
        out_specs=tok,
        out_shape=jax.ShapeDtypeStruct((T, D_MODEL), F32),
        compiler_params=_params(1),
        name="mlp",
    )(*args)


PROJ_TM = 512


def _inproj_kernel(x_ref, g_ref, w_ref, *o_refs, splits):
    h = _rms(x_ref[...], g_ref[...]).astype(BF16)
    for o_ref, (c0, c1) in zip(o_refs, splits):
        o_ref[...] = _dot(h, w_ref[:, c0:c1]).astype(o_ref.dtype)


def _inproj(x2, g, w, splits, dtypes):
    T = x2.shape[0]
    n_out = w.shape[1]
    tok = pl.BlockSpec((PROJ_TM, D_MODEL), lambda i: (i, 0))
    return pl.pallas_call(
        functools.partial(_inproj_kernel, splits=splits),
        grid=(T // PROJ_TM,),
        in_specs=[tok, _const_spec((1, D_MODEL)), _const_spec((D_MODEL, n_out))],
        out_specs=[pl.BlockSpec((PROJ_TM, c1 - c0), lambda i: (i, 0)) for c0, c1 in splits],
        out_shape=[jax.ShapeDtypeStruct((T, c1 - c0), dt) for (c0, c1), dt in zip(splits, dtypes)],
        compiler_params=_params(1),
        name="inproj",
    )(x2, g.reshape(1, D_MODEL), w)


def _outproj_kernel(x_ref, y_ref, w_ref, o_ref):
    o_ref[...] = x_ref[...] + _dot(y_ref[...], w_ref[...])


def _outproj(x2, y2, w):
    T = x2.shape[0]
    tok = pl.BlockSpec((PROJ_TM, D_MODEL), lambda i: (i, 0))
    return pl.pallas_call(
        _outproj_kernel,
        grid=(T // PROJ_TM,),
        in_specs=[tok, tok, _const_spec((D_MODEL, D_MODEL))],
        out_specs=tok,
        out_shape=jax.ShapeDtypeStruct((T, D_MODEL), F32),
        compiler_params=_params(1),
        name="outproj",
    )(x2, y2, w)


SGU_TM = 512


def _sgu_kernel(x_ref, g_ref, win_ref, lng_ref, lnb_ref, ws_ref, bs_ref, wout_ref, o_ref, y_ref):
    x = x_ref[...]
    h = _rms(x, g_ref[...]).astype(BF16)
    u = _gelu_tanh(_dot(h, win_ref[:, :D_MODEL]))
    v = _gelu_tanh(_dot(h, win_ref[:, D_MODEL:]))
    mu = jnp.mean(v, axis=-1, keepdims=True)
    vc = v - mu
    var = jnp.mean(vc * vc, axis=-1, keepdims=True)
    v = (vc * lax.rsqrt(var + 1e-5) * lng_ref[...] + lnb_ref[...]).astype(BF16)
    n_chunks = SGU_TM // SGU_CHUNK
    for gi in range(GB):
        cols = slice(gi * DGB, (gi + 1) * DGB)
        vg = jnp.concatenate([v[c * SGU_CHUNK:(c + 1) * SGU_CHUNK, cols] for c in range(n_chunks)], axis=1)
        vs = _dot(ws_ref[gi], vg)
        for c in range(n_chunks):
            rows = slice(c * SGU_CHUNK, (c + 1) * SGU_CHUNK)
            vsc = vs[:, c * DGB:(c + 1) * DGB] + bs_ref[gi]
            y_ref[rows, cols] = (u[rows, cols] * vsc).astype(BF16)
    o_ref[...] = x + _dot(y_ref[...], wout_ref[...])


def _sgu(x2, g, w_in, ln_g, ln_b, w_s, b_s, w_out):
    T = x2.shape[0]
    tok = pl.BlockSpec((SGU_TM, D_MODEL), lambda i: (i, 0))
    bs = jnp.broadcast_to(b_s[:, :, None], (GB, SGU_CHUNK, DGB)).astype(F32)
    return pl.pallas_call(
        _sgu_kernel,
        grid=(T // SGU_TM,),
        in_specs=[tok, _const_spec((1, D_MODEL)), _const_spec((D_MODEL, 2 * D_MODEL)),
                  _const_spec((1, D_MODEL)), _const_spec((1, D_MODEL)),
                  _const_spec((GB, SGU_CHUNK, SGU_CHUNK)), _const_spec((GB, SGU_CHUNK, DGB)),
                  _const_spec((D_MODEL, D_MODEL))],
        out_specs=tok,
        out_shape=jax.ShapeDtypeStruct((T, D_MODEL), F32),
        scratch_shapes=[pltpu.VMEM((SGU_TM, D_MODEL), BF16)],
        compiler_params=_params(1),
        name="sgu",
    )(x2, g.reshape(1, D_MODEL), w_in.astype(BF16), ln_g.reshape(1, D_MODEL), ln_b.reshape(1, D_MODEL),
      w_s.astype(BF16), bs, w_out.astype(BF16))


def _rglru_kernel(xr_ref, gate_ref, cw_ref, cb_ref, gw_ref, gb_ref, lam_ref, o_ref,
                  xpad_ref, a_ref, b_ref, h_ref, p_ref, *, seq):
    seg = seq // SUBLANES
    pitch = seg + SEG_PAD
    pad = SUBLANES

    zeros = jnp.zeros((pad, HDA), F32)
    xpad_ref[0:pad, :] = zeros
    xpad_ref[pad + seq:pad + seq + pad, :] = zeros
    xpad_ref[pad:pad + seq, :] = xr_ref[0]

    cw = cw_ref[...]
    cb = cb_ref[...]
    sp = [_softplus(-lam_ref[d:d + 1, :]) for d in range(2)]

    for s in range(SUBLANES):
        r0 = s * seg
        xc = cb
        for k in range(CONV_W):
            xc = xc + xpad_ref[pl.ds(pad + r0 + k - CONV_LEFT, seg), :] * cw[k:k + 1, :]
        xb = xc.astype(BF16)
        for d in range(2):
            r = _sigmoid(_dot(xb, gw_ref[d, 0, 0]) + gb_ref[d, 0])
            ig = _sigmoid(_dot(xb, gw_ref[d, 1, 0]) + gb_ref[d, 1])
            log_a = (-RG_C) * r * sp[d]
            a = jnp.exp(log_a)
            mult = jnp.sqrt(1.0 - a * a)
            a_ref[d, s * pitch:s * pitch + seg, :] = a
            b_ref[d, s * pitch:s * pitch + seg, :] = mult * ig * xc

    def scan_step(i, carry):
        hf, pf, hb, pb = carry
        j = seg - 1 - i
        af = a_ref[0, pl.ds(i, SUBLANES, stride=pitch), :]
        bf = b_ref[0, pl.ds(i, SUBLANES, stride=pitch), :]
        ab = a_ref[1, pl.ds(j, SUBLANES, stride=pitch), :]
        bb = b_ref[1, pl.ds(j, SUBLANES, stride=pitch), :]
        hf = af * hf + bf
        pf = af * pf
        hb = ab * hb + bb
        pb = ab * pb
        h_ref[0, pl.ds(i, SUBLANES, stride=pitch), :] = hf
        p_ref[0, pl.ds(i, SUBLANES, stride=pitch), :] = pf
        h_ref[1, pl.ds(j, SUBLANES, stride=pitch), :] = hb
        p_ref[1, pl.ds(j, SUBLANES, stride=pitch), :] = pb
        return hf, pf, hb, pb

    z = jnp.zeros((SUBLANES, HDA), F32)
    o = jnp.ones((SUBLANES, HDA), F32)
    hf, pf, hb, pb = lax.fori_loop(0, seg, scan_step, (z, o, z, o), unroll=8)

    row = lax.broadcasted_iota(jnp.int32, (SUBLANES, HDA), 0)
    cf = z
    cr = z
    for _ in range(SUBLANES - 1):
        cf = jnp.where(row == 0, 0.0, pltpu.roll(hf + pf * cf, 1, axis=0))
        cr = jnp.where(row == SUBLANES - 1, 0.0, pltpu.roll(hb + pb * cr, SUBLANES - 1, axis=0))

    for s in range(SUBLANES):
        rows = slice(s * pitch, s * pitch + seg)
        yf = h_ref[0, rows, :] + p_ref[0, rows, :] * cf[s:s + 1, :]
        yb = h_ref[1, rows, :] + p_ref[1, rows, :] * cr[s:s + 1, :]
        gate = gate_ref[0, s * seg:(s + 1) * seg, :].astype(F32)
        o_ref[0, s * seg:(s + 1) * seg, :] = ((yf + yb) * _gelu_tanh(gate)).astype(o_ref.dtype)


def _rglru_core(gate3, xr3, conv_w, conv_b, gate_w, gate_b, lam):
    B, S, _ = xr3.shape
    seg = S // SUBLANES
    rows = SUBLANES * (seg + SEG_PAD)
    blk = pl.BlockSpec((1, S, HDA), lambda b, h: (b, 0, h))
    return pl.pallas_call(
        functools.partial(_rglru_kernel, seq=S),
        grid=(B, HA),
        in_specs=[blk, blk,
                  pl.BlockSpec((CONV_W, HDA), lambda b, h: (0, h)),
                  pl.BlockSpec((1, HDA), lambda b, h: (0, h)),
                  pl.BlockSpec((2, 2, 1, HDA, HDA), lambda b, h: (0, 0, h, 0, 0)),
                  pl.BlockSpec((2, 2, 1, HDA), lambda b, h: (0, 0, 0, h)),
                  pl.BlockSpec((2, HDA), lambda b, h: (0, h))],
        out_specs=blk,
        out_shape=jax.ShapeDtypeStruct((B, S, D_MODEL), BF16),
        scratch_shapes=[pltpu.VMEM((S + 2 * SUBLANES, HDA), F32),
                        pltpu.VMEM((2, rows, HDA), F32), pltpu.VMEM((2, rows, HDA), F32),
                        pltpu.VMEM((2, rows, HDA), F32), pltpu.VMEM((2, rows, HDA), F32)],
        compiler_params=_params(2),
        name="rglru",
    )(xr3, gate3, conv_w, conv_b.reshape(1, D_MODEL), gate_w.astype(BF16),
      gate_b.reshape(2, 2, 1, D_MODEL), lam)


def _mixer_rglru(x2, B, S, g, w_in, conv_w, conv_b, gate_w, gate_b, lam, w_out):
    gate, xr = _inproj(x2, g, w_in.astype(BF16), ((0, D_MODEL), (D_MODEL, 2 * D_MODEL)), (BF16, F32))
    y = _rglru_core(gate.reshape(B, S, D_MODEL), xr.reshape(B, S, D_MODEL),
                    conv_w, conv_b, gate_w, gate_b, lam)
    return _outproj(x2, y.reshape(B * S, D_MODEL), w_out.astype(BF16))


ROW_BETA, ROW_GC, ROW_EG, ROW_EKD, ROW_ETOT = 2, 8, 16, 24, 32


def _inproj_t_kernel(x_ref, g_ref, w_ref, wt_ref, *o_refs, splits):
    h = _rms(x_ref[...], g_ref[...]).astype(BF16)
    for o_ref, (c0, c1) in zip(o_refs[:-1], splits):
        o_ref[...] = _dot(h, w_ref[:, c0:c1]).astype(o_ref.dtype)
    o_refs[-1][...] = _dot_nt(wt_ref[...], h)


def _inproj_t(x2, g, w, wt, splits, dtypes):
    T = x2.shape[0]
    n_out = w.shape[1]
    n_t = wt.shape[0]
    tok = pl.BlockSpec((PROJ_TM, D_MODEL), lambda i: (i, 0))
    return pl.pallas_call(
        functools.partial(_inproj_t_kernel, splits=splits),
        grid=(T // PROJ_TM,),
        in_specs=[tok, _const_spec((1, D_MODEL)), _const_spec((D_MODEL, n_out)), _const_spec((n_t, D_MODEL))],
        out_specs=[pl.BlockSpec((PROJ_TM, c1 - c0), lambda i: (i, 0)) for c0, c1 in splits]
        + [pl.BlockSpec((n_t, PROJ_TM), lambda i: (0, i))],
        out_shape=[jax.ShapeDtypeStruct((T, c1 - c0), dt) for (c0, c1), dt in zip(splits, dtypes)]
        + [jax.ShapeDtypeStruct((n_t, T), F32)],
        compiler_params=_params(1),
        name="inproj_t",
    )(x2, g.reshape(1, D_MODEL), w, wt)


def _gdn_kernel(q_ref, k_ref, v_ref, gate_ref, ab_ref, alog_ref, dtb_ref, cwq_ref, cwk_ref, cwv_ref, ng_ref,
                o_ref,
                xpad_ref, qn_ref, kn_ref, vv_ref, cf_ref, gcrow_ref, u_ref, wq_ref, qkd_ref, kdt_ref, of_ref,
                *, seq):
    pad = SUBLANES
    n_blk = seq // GDN_BLOCK
    C = GDN_CHUNK
    RB = 256

    zeros = jnp.zeros((pad, DK), F32)
    xpad_ref[0:pad, :] = zeros
    xpad_ref[pad + seq:pad + seq + pad, :] = zeros

    def conv_silu(src_ref, cw_ref, dst_ref, normalize, scale):
        xpad_ref[pad:pad + seq, :] = src_ref[0]
        cw = cw_ref[...]
        for rb in range(seq // RB):
            r0 = rb * RB
            acc = xpad_ref[pl.ds(pad + r0 - CONV_LEFT, RB), :] * cw[0:1, :]
            for k in range(1, CONV_W):
                acc = acc + xpad_ref[pl.ds(pad + r0 + k - CONV_LEFT, RB), :] * cw[k:k + 1, :]
            y = _silu(acc)
            if normalize:
                y = y * (lax.rsqrt(jnp.sum(y * y, axis=-1, keepdims=True) + 1e-6) * scale)
            dst_ref[r0:r0 + RB, :] = y

    conv_silu(q_ref, cwq_ref, qn_ref, True, DK ** -0.5)
    conv_silu(k_ref, cwk_ref, kn_ref, True, 1.0)
    conv_silu(v_ref, cwv_ref, vv_ref, False, 1.0)

    ab = ab_ref[0]
    g8 = -jnp.exp(alog_ref[0][:, 0:1]) * _softplus(ab + dtb_ref[0][:, 0:1])
    beta8 = _sigmoid(ab)
    lane = lax.broadcasted_iota(jnp.int32, (SUBLANES, seq), 1) % C
    pre = g8
    suf = g8
    sh = 1
    while sh < C:
        pre = pre + jnp.where(lane >= sh, pltpu.roll(pre, sh, axis=1), 0.0)
        suf = suf + jnp.where(lane < C - sh, pltpu.roll(suf, seq - sh, axis=1), 0.0)
        sh *= 2
    tot = pre + suf - g8
    row8 = lax.broadcasted_iota(jnp.int32, (SUBLANES, seq), 0)
    gcs = jnp.where(row8 == 0, pre, suf)
    eg = jnp.exp(gcs)
    ekd = jnp.exp(tot - gcs)
    etot = jnp.exp(tot)
    fill = jnp.zeros((GDN_BLOCK - 5 * SUBLANES, GDN_BLOCK), F32)
    for b in range(n_blk):
        ls = slice(b * GDN_BLOCK, (b + 1) * GDN_BLOCK)
        table = jnp.concatenate([beta8[:, ls], gcs[:, ls], eg[:, ls], ekd[:, ls], etot[:, ls], fill], axis=0)
        cf_ref[b * GDN_BLOCK:(b + 1) * GDN_BLOCK, :] = table.T
        gcrow_ref[b] = gcs[:, ls]

    ri = lax.broadcasted_iota(jnp.int32, (GDN_BLOCK, GDN_BLOCK), 0)
    ci = lax.broadcasted_iota(jnp.int32, (GDN_BLOCK, GDN_BLOCK), 1)
    same = (ri // C) == (ci // C)
    masks = (same & (ri >= ci), same & (ri <= ci))
    stricts = (same & (ri > ci), same & (ri < ci))
    eye = (ri == ci).astype(F32)

    def local_step(b, carry):
        r0 = pl.multiple_of(b * GDN_BLOCK, GDN_BLOCK)
        kn = kn_ref[pl.ds(r0, GDN_BLOCK), :]
        qn = qn_ref[pl.ds(r0, GDN_BLOCK), :]
        vv = vv_ref[pl.ds(r0, GDN_BLOCK), :]
        cf = cf_ref[pl.ds(r0, GDN_BLOCK), :]
        gcr = gcrow_ref[b]
        kq = _dot_nt(jnp.concatenate([kn, qn], axis=0).astype(BF16), kn.astype(BF16))
        kk = kq[:GDN_BLOCK]
        qk = kq[GDN_BLOCK:]
        for d in range(2):
            beta_c = cf[:, ROW_BETA + d:ROW_BETA + d + 1]
            gc_c = cf[:, ROW_GC + d:ROW_GC + d + 1]
            eg_c = cf[:, ROW_EG + d:ROW_EG + d + 1]
            ekd_c = cf[:, ROW_EKD + d:ROW_EKD + d + 1]
            decay = jnp.where(masks[d], jnp.exp(jnp.where(masks[d], gc_c - gcr[d:d + 1, :], 0.0)), 0.0)
            n_mat = jnp.where(stricts[d], -(beta_c * kk * decay), 0.0)
            t_mat = eye + n_mat
            p16 = n_mat.astype(BF16)
            for j in range(5):
                p = _dot(p16, p16)
                p16 = p.astype(BF16)
                t_mat = t_mat + _dot(t_mat.astype(BF16), p16)
            rhs = jnp.concatenate([vv * beta_c, kn * (beta_c * eg_c)], axis=1).astype(BF16)
            uw = _dot(t_mat.astype(BF16), rhs)
            u = uw[:, :DV]
            w = uw[:, DV:].astype(BF16)
            qg = (qn * eg_c).astype(BF16)
            u_ref[d, pl.ds(r0, GDN_BLOCK), :] = u
            for c in range(2):
                wq_ref[d, 2 * b + c] = jnp.concatenate([w[c * C:(c + 1) * C], qg[c * C:(c + 1) * C]], axis=0)
            qkd_ref[d, pl.ds(r0, GDN_BLOCK), :] = jnp.where(masks[d], qk * decay, 0.0).astype(BF16)
            kdt_ref[d, b] = (kn * ekd_c).T.astype(BF16)
        return carry

    lax.fori_loop(0, n_blk, local_step, 0)

    zc = jnp.zeros((C, DV), F32)

    def chunk(d, b, c, state):
        n = 2 * b + c
        r0 = pl.multiple_of(n * C, C)
        wq = _dot(wq_ref[d, n], state.astype(BF16))
        v_new = u_ref[d, pl.ds(r0, C), :] - wq[:C]
        vp = jnp.concatenate([v_new, zc] if c == 0 else [zc, v_new], axis=0).astype(BF16)
        o = wq[C:] + _dot(qkd_ref[d, pl.ds(r0, C), :], vp)
        et = cf_ref[pl.ds(r0, 1), ROW_ETOT + d:ROW_ETOT + d + 1]
        state = state * et + _dot(kdt_ref[d, b], vp)
        return o, r0, state

    def state_step(m, carry):
        sf, sb = carry
        for c in range(2):
            o, r0, sf = chunk(0, m, c, sf)
            of_ref[0, pl.ds(r0, C), :] = o
            o, r0, sb = chunk(1, n_blk - 1 - m, 1 - c, sb)
            of_ref[1, pl.ds(r0, C), :] = o
        return sf, sb

    s0 = jnp.zeros((DK, DV), F32)
    lax.fori_loop(0, n_blk, state_step, (s0, s0))

    ng = ng_ref[...]
    for rb in range(seq // RB):
        rows = slice(rb * RB, (rb + 1) * RB)
        o = of_ref[0, rows, :] + of_ref[1, rows, :]
        gate = gate_ref[0, rows, :].astype(F32)
        o_ref[0, rows, :] = (_rms(o, ng) * _silu(gate)).astype(o_ref.dtype)


def _gdn_core(qkv3, gate3, abt3, alog8, dtb8, conv_w, norm_g):
    B, S, _ = qkv3.shape
    n_blk = S // GDN_BLOCK
    n_chunks = S // GDN_CHUNK

    def col(off):
        return pl.BlockSpec((1, S, DK), lambda b, h: (b, 0, off + h))

    def cw(off):
        return pl.BlockSpec((CONV_W, DK), lambda b, h: (0, off + h))

    par = pl.BlockSpec((1, SUBLANES, LANES), lambda b, h: (h, 0, 0))
    return pl.pallas_call(
        functools.partial(_gdn_kernel, seq=S),
        grid=(B, HC),
        in_specs=[col(0), col(HC), col(2 * HC),
                  pl.BlockSpec((1, S, DV), lambda b, h: (b, 0, h)),
                  pl.BlockSpec((1, SUBLANES, S), lambda b, h: (h, 0, b)),
                  par, par, cw(0), cw(HC), cw(2 * HC),
                  pl.BlockSpec((1, DV), lambda b, h: (0, 0))],
        out_specs=pl.BlockSpec((1, S, DV), lambda b, h: (b, 0, h)),
        out_shape=jax.ShapeDtypeStruct((B, S, DC), BF16),
        scratch_shapes=[pltpu.VMEM((S + 2 * SUBLANES, DK), F32),
                        pltpu.VMEM((S, DK), F32), pltpu.VMEM((S, DK), F32), pltpu.VMEM((S, DV), F32),
                        pltpu.VMEM((S, LANES), F32), pltpu.VMEM((n_blk, SUBLANES, GDN_BLOCK), F32),
                        pltpu.VMEM((2, S, DV), F32), pltpu.VMEM((2, n_chunks, GDN_BLOCK, DK), BF16),
                        pltpu.VMEM((2, S, GDN_BLOCK), BF16), pltpu.VMEM((2, n_blk, DK, GDN_BLOCK), BF16),
                        pltpu.VMEM((2, S, DV), F32)],
        compiler_params=_params(2),
        name="gdn",
    )(qkv3, qkv3, qkv3, gate3, abt3, alog8, dtb8, conv_w, conv_w, conv_w, norm_g.reshape(1, DV))


def _mixer_gdn(x2, B, S, g, w_in, conv_w, a_log, dt_bias, norm_g, w_out):
    w_ab = w_in[:, 4 * DC:].reshape(D_MODEL, 2, 2, HC).transpose(3, 1, 2, 0).reshape(HC, 4, D_MODEL)
    w_ab = jnp.concatenate([w_ab, jnp.zeros_like(w_ab)], axis=1).reshape(HC * SUBLANES, D_MODEL)
    qkv, gate, abt = _inproj_t(x2, g, w_in[:, :4 * DC].astype(BF16), w_ab.astype(BF16),
                               ((0, 3 * DC), (3 * DC, 4 * DC)), (F32, BF16))

    def per_head(p):
        p8 = jnp.concatenate([p.T.astype(F32), jnp.zeros((HC, SUBLANES - 2), F32)], axis=1)
        return jnp.broadcast_to(p8[:, :, None], (HC, SUBLANES, LANES))

    y = _gdn_core(qkv.reshape(B, S, 3 * DC), gate.reshape(B, S, DC), abt.reshape(HC, SUBLANES, B * S),
                  per_head(a_log), per_head(dt_bias), conv_w, norm_g)
    return _outproj(x2, y.reshape(B * S, DC), w_out.astype(BF16))


def kernel(x, norm_mix_g, norm_mlp_g, mlp_w_up, mlp_w_down, norm_final_g, a_w_in, a_conv_w, a_conv_b, a_gate_w, a_gate_b, a_lambda, a_w_out, b_w_in, b_ln_g, b_ln_b, b_w_s, b_b_s, b_w_out, c_w_in, c_conv_w, c_a_log, c_dt_bias, c_norm_g, c_w_out):
    B, S, _ = x.shape
    depth = norm_mix_g.shape[0]
    x2 = x.reshape(B * S, D_MODEL)
    for i in range(depth):
        kind, j = i % N_MIXERS, i // N_MIXERS
        g = norm_mix_g[i]
        if kind == 0:
            x2 = _mixer_rglru(x2, B, S, g, a_w_in[j], a_conv_w[j], a_conv_b[j], a_gate_w[j], a_gate_b[j],
                              a_lambda[j], a_w_out[j])
        elif kind == 1:
            x2 = _sgu(x2, g, b_w_in[j], b_ln_g[j], b_ln_b[j], b_w_s[j], b_b_s[j], b_w_out[j])
        else:
            x2 = _mixer_gdn(x2, B, S, g, c_w_in[j], c_conv_w[j], c_a_log[j], c_dt_bias[j], c_norm_g[j], c_w_out[j])
        x2 = _mlp(x2, norm_mlp_g[i], mlp_w_up[i].astype(BF16), mlp_w_down[i].astype(BF16),
                  norm_final_g if i == depth - 1 else None)
    return x2.reshape(B, S, D_MODEL)
```

```python
import functools

import jax
import jax.numpy as jnp
from jax import lax
from jax.experimental import pallas as pl
from jax.experimental.pallas import tpu as pltpu

F32 = jnp.float32
BF16 = jnp.bfloat16

D_MODEL = 1024
D_FF = 4 * D_MODEL
N_MIXERS = 3
LANES = 128
SUBLANES = 8
VMEM_LIMIT = 56 * 1024 * 1024

HA = 8
HDA = D_MODEL // HA
CONV_W = 4
CONV_LEFT = CONV_W // 2
RG_C = 8.0
SEG_PAD = 8
SGU_CHUNK = 128
GB = 8
DGB = D_MODEL // GB
HC = 8
DK = 128
DV = 128
DC = HC * DV
GDN_CHUNK = 64
GDN_BLOCK = 2 * GDN_CHUNK
GDN_INV_BASE = 8
GDN_LOCAL_UNROLL = 4


def _params(n_parallel):
    return pltpu.CompilerParams(dimension_semantics=("parallel",) * n_parallel,
                                vmem_limit_bytes=VMEM_LIMIT)


def _rms(xf, g, eps=1e-6):
    return xf * lax.rsqrt(jnp.mean(xf * xf, axis=-1, keepdims=True) + eps) * g


def _sigmoid(x):
    return 1.0 / (1.0 + jnp.exp(-x))


def _silu(x):
    return x * _sigmoid(x)


def _gelu_tanh(x):
    c = 0.7978845608028654
    return 0.5 * x * (1.0 + jnp.tanh(c * (x + 0.044715 * (x * x * x))))


def _softplus(x):
    return jnp.maximum(x, 0.0) + jnp.log1p(jnp.exp(-jnp.abs(x)))


def _dot(a, b):
    return jnp.dot(a, b, preferred_element_type=F32)


def _dot_nt(a, b):
    return lax.dot_general(a, b, (((1,), (1,)), ((), ())), preferred_element_type=F32)


MLP_TM = 512
MLP_FC = 1024


def _mlp_kernel(x_ref, g_ref, wup_ref, wdn_ref, *rest, final_norm):
    if final_norm:
        gf_ref, o_ref = rest
    else:
        (o_ref,) = rest
    x = x_ref[...]
    h = _rms(x, g_ref[...]).astype(BF16)
    acc = x
    for c in range(D_FF // MLP_FC):
        u = _dot(h, wup_ref[:, c * MLP_FC:(c + 1) * MLP_FC])
        a = jnp.square(jnp.maximum(u, 0.0)).astype(BF16)
        acc = acc + _dot(a, wdn_ref[c * MLP_FC:(c + 1) * MLP_FC, :])
    if final_norm:
        acc = _rms(acc, gf_ref[...])
    o_ref[...] = acc


def _const_spec(shape):
    n = len(shape)
    return pl.BlockSpec(shape, lambda *_: (0,) * n, pipeline_mode=pl.Buffered(1))


def _mlp(x2, g, w_up, w_down, g_final=None):
    T = x2.shape[0]
    tok = pl.BlockSpec((MLP_TM, D_MODEL), lambda i: (i, 0))
    in_specs = [tok, _const_spec((1, D_MODEL)), _const_spec((D_MODEL, D_FF)), _const_spec((D_FF, D_MODEL))]
    args = [x2, g.reshape(1, D_MODEL), w_up, w_down]
    if g_final is not None:
        in_specs.append(_const_spec((1, D_MODEL)))
        args.append(g_final.reshape(1, D_MODEL))
    return pl.pallas_call(
        functools.partial(_mlp_kernel, final_norm=g_final is not None),
        grid=(T // MLP_TM,),
        in_specs=in_specs,
        out_specs=tok,
        out_shape=jax.ShapeDtypeStruct((T, D_MODEL), F32),
        compiler_params=_params(1),
        name="mlp",
    )(*args)


PROJ_TM = 512


def _inproj_kernel(x_ref, g_ref, w_ref, *o_refs, splits):
    h = _rms(x_ref[...], g_ref[...]).astype(BF16)
    for o_ref, (c0, c1) in zip(o_refs, splits):
        o_ref[...] = _dot(h, w_ref[:, c0:c1]).astype(o_ref.dtype)


def _inproj(x2, g, w, splits, dtypes):
    T = x2.shape[0]
    n_out = w.shape[1]
    tok = pl.BlockSpec((PROJ_TM, D_MODEL), lambda i: (i, 0))
    return pl.pallas_call(
        functools.partial(_inproj_kernel, splits=splits),
        grid=(T // PROJ_TM,),
        in_specs=[tok, _const_spec((1, D_MODEL)), _const_spec((D_MODEL, n_out))],
        out_specs=[pl.BlockSpec((PROJ_TM, c1 - c0), lambda i: (i, 0)) for c0, c1 in splits],
        out_shape=[jax.ShapeDtypeStruct((T, c1 - c0), dt) for (c0, c1), dt in zip(splits, dtypes)],
        compiler_params=_params(1),
        name="inproj",
    )(x2, g.reshape(1, D_MODEL), w)


def _outproj_kernel(x_ref, y_ref, w_ref, o_ref):
    o_ref[...] = x_ref[...] + _dot(y_ref[...], w_ref[...])


def _outproj(x2, y2, w):
    T = x2.shape[0]
    tok = pl.BlockSpec((PROJ_TM, D_MODEL), lambda i: (i, 0))
    return pl.pallas_call(
        _outproj_kernel,
        grid=(T // PROJ_TM,),
        in_specs=[tok, tok, _const_spec((D_MODEL, D_MODEL))],
        out_specs=tok,
        out_shape=jax.ShapeDtypeStruct((T, D_MODEL), F32),
        compiler_params=_params(1),
        name="outproj",
    )(x2, y2, w)


SGU_TM = 512


def _sgu_kernel(x_ref, g_ref, win_ref, lng_ref, lnb_ref, ws_ref, bs_ref, wout_ref, o_ref, y_ref):
    x = x_ref[...]
    h = _rms(x, g_ref[...]).astype(BF16)
    u = _gelu_tanh(_dot(h, win_ref[:, :D_MODEL]))
    v = _gelu_tanh(_dot(h, win_ref[:, D_MODEL:]))
    mu = jnp.mean(v, axis=-1, keepdims=True)
    vc = v - mu
    var = jnp.mean(vc * vc, axis=-1, keepdims=True)
    v = (vc * lax.rsqrt(var + 1e-5) * lng_ref[...] + lnb_ref[...]).astype(BF16)
    n_chunks = SGU_TM // SGU_CHUNK
    for gi in range(GB):
        cols = slice(gi * DGB, (gi + 1) * DGB)
        vg = jnp.concatenate([v[c * SGU_CHUNK:(c + 1) * SGU_CHUNK, cols] for c in range(n_chunks)], axis=1)
        vs = _dot(ws_ref[gi], vg)
        for c in range(n_chunks):
            rows = slice(c * SGU_CHUNK, (c + 1) * SGU_CHUNK)
            vsc = vs[:, c * DGB:(c + 1) * DGB] + bs_ref[gi]
            y_ref[rows, cols] = (u[rows, cols] * vsc).astype(BF16)
    o_ref[...] = x + _dot(y_ref[...], wout_ref[...])


def _sgu(x2, g, w_in, ln_g, ln_b, w_s, b_s, w_out):
    T = x2.shape[0]
    tok = pl.BlockSpec((SGU_TM, D_MODEL), lambda i: (i, 0))
    bs = jnp.broadcast_to(b_s[:, :, None], (GB, SGU_CHUNK, DGB)).astype(F32)
    return pl.pallas_call(
        _sgu_kernel,
        grid=(T // SGU_TM,),
        in_specs=[tok, _const_spec((1, D_MODEL)), _const_spec((D_MODEL, 2 * D_MODEL)),
                  _const_spec((1, D_MODEL)), _const_spec((1, D_MODEL)),
                  _const_spec((GB, SGU_CHUNK, SGU_CHUNK)), _const_spec((GB, SGU_CHUNK, DGB)),
                  _const_spec((D_MODEL, D_MODEL))],
        out_specs=tok,
        out_shape=jax.ShapeDtypeStruct((T, D_MODEL), F32),
        scratch_shapes=[pltpu.VMEM((SGU_TM, D_MODEL), BF16)],
        compiler_params=_params(1),
        name="sgu",
    )(x2, g.reshape(1, D_MODEL), w_in.astype(BF16), ln_g.reshape(1, D_MODEL), ln_b.reshape(1, D_MODEL),
      w_s.astype(BF16), bs, w_out.astype(BF16))


def _rglru_kernel(xr_ref, gate_ref, cw_ref, cb_ref, gw_ref, gb_ref, lam_ref, o_ref,
                  xpad_ref, a_ref, b_ref, h_ref, p_ref, *, seq):
    seg = seq // SUBLANES
    pitch = seg + SEG_PAD
    pad = SUBLANES

    zeros = jnp.zeros((pad, HDA), F32)
    xpad_ref[0:pad, :] = zeros
    xpad_ref[pad + seq:pad + seq + pad, :] = zeros
    xpad_ref[pad:pad + seq, :] = xr_ref[0]

    cw = cw_ref[...]
    cb = cb_ref[...]
    sp = [_softplus(-lam_ref[d:d + 1, :]) for d in range(2)]

    for s in range(SUBLANES):
        r0 = s * seg
        xc = cb
        for k in range(CONV_W):
            xc = xc + xpad_ref[pl.ds(pad + r0 + k - CONV_LEFT, seg), :] * cw[k:k + 1, :]
        xb = xc.astype(BF16)
        for d in range(2):
            r = _sigmoid(_dot(xb, gw_ref[d, 0, 0]) + gb_ref[d, 0])
            ig = _sigmoid(_dot(xb, gw_ref[d, 1, 0]) + gb_ref[d, 1])
            log_a = (-RG_C) * r * sp[d]
            a = jnp.exp(log_a)
            mult = jnp.sqrt(1.0 - a * a)
            a_ref[d, s * pitch:s * pitch + seg, :] = a
            b_ref[d, s * pitch:s * pitch + seg, :] = mult * ig * xc

    def scan_step(i, carry):
        hf, pf, hb, pb = carry
        j = seg - 1 - i
        af = a_ref[0, pl.ds(i, SUBLANES, stride=pitch), :]
        bf = b_ref[0, pl.ds(i, SUBLANES, stride=pitch), :]
        ab = a_ref[1, pl.ds(j, SUBLANES, stride=pitch), :]
        bb = b_ref[1, pl.ds(j, SUBLANES, stride=pitch), :]
        hf = af * hf + bf
        pf = af * pf
        hb = ab * hb + bb
        pb = ab * pb
        h_ref[0, pl.ds(i, SUBLANES, stride=pitch), :] = hf
        p_ref[0, pl.ds(i, SUBLANES, stride=pitch), :] = pf
        h_ref[1, pl.ds(j, SUBLANES, stride=pitch), :] = hb
        p_ref[1, pl.ds(j, SUBLANES, stride=pitch), :] = pb
        return hf, pf, hb, pb

    z = jnp.zeros((SUBLANES, HDA), F32)
    o = jnp.ones((SUBLANES, HDA), F32)
    hf, pf, hb, pb = lax.fori_loop(0, seg, scan_step, (z, o, z, o), unroll=8)

    row = lax.broadcasted_iota(jnp.int32, (SUBLANES, HDA), 0)
    cf = z
    cr = z
    for _ in range(SUBLANES - 1):
        cf = jnp.where(row == 0, 0.0, pltpu.roll(hf + pf * cf, 1, axis=0))
        cr = jnp.where(row == SUBLANES - 1, 0.0, pltpu.roll(hb + pb * cr, SUBLANES - 1, axis=0))

    for s in range(SUBLANES):
        rows = slice(s * pitch, s * pitch + seg)
        yf = h_ref[0, rows, :] + p_ref[0, rows, :] * cf[s:s + 1, :]
        yb = h_ref[1, rows, :] + p_ref[1, rows, :] * cr[s:s + 1, :]
        gate = gate_ref[0, s * seg:(s + 1) * seg, :].astype(F32)
        o_ref[0, s * seg:(s + 1) * seg, :] = ((yf + yb) * _gelu_tanh(gate)).astype(o_ref.dtype)


def _rglru_core(gate3, xr3, conv_w, conv_b, gate_w, gate_b, lam):
    B, S, _ = xr3.shape
    seg = S // SUBLANES
    rows = SUBLANES * (seg + SEG_PAD)
    blk = pl.BlockSpec((1, S, HDA), lambda b, h: (b, 0, h))
    return pl.pallas_call(
        functools.partial(_rglru_kernel, seq=S),
        grid=(B, HA),
        in_specs=[blk, blk,
                  pl.BlockSpec((CONV_W, HDA), lambda b, h: (0, h)),
                  pl.BlockSpec((1, HDA), lambda b, h: (0, h)),
                  pl.BlockSpec((2, 2, 1, HDA, HDA), lambda b, h: (0, 0, h, 0, 0)),
                  pl.BlockSpec((2, 2, 1, HDA), lambda b, h: (0, 0, 0, h)),
                  pl.BlockSpec((2, HDA), lambda b, h: (0, h))],
        out_specs=blk,
        out_shape=jax.ShapeDtypeStruct((B, S, D_MODEL), BF16),
        scratch_shapes=[pltpu.VMEM((S + 2 * SUBLANES, HDA), F32),
                        pltpu.VMEM((2, rows, HDA), F32), pltpu.VMEM((2, rows, HDA), F32),
                        pltpu.VMEM((2, rows, HDA), F32), pltpu.VMEM((2, rows, HDA), F32)],
        compiler_params=_params(2),
        name="rglru",
    )(xr3, gate3, conv_w, conv_b.reshape(1, D_MODEL), gate_w.astype(BF16),
      gate_b.reshape(2, 2, 1, D_MODEL), lam)


def _mixer_rglru(x2, B, S, g, w_in, conv_w, conv_b, gate_w, gate_b, lam, w_out):
    gate, xr = _inproj(x2, g, w_in.astype(BF16), ((0, D_MODEL), (D_MODEL, 2 * D_MODEL)), (BF16, F32))
    y = _rglru_core(gate.reshape(B, S, D_MODEL), xr.reshape(B, S, D_MODEL),
                    conv_w, conv_b, gate_w, gate_b, lam)
    return _outproj(x2, y.reshape(B * S, D_MODEL), w_out.astype(BF16))


ROW_BETA, ROW_GC, ROW_EG, ROW_EKD, ROW_ETOT = 2, 8, 16, 24, 32


def _inproj_t_kernel(x_ref, g_ref, w_ref, wt_ref, *o_refs, splits):
    h = _rms(x_ref[...], g_ref[...]).astype(BF16)
    for o_ref, (c0, c1) in zip(o_refs[:-1], splits):
        o_ref[...] = _dot(h, w_ref[:, c0:c1]).astype(o_ref.dtype)
    o_refs[-1][...] = _dot_nt(wt_ref[...], h)


def _inproj_t(x2, g, w, wt, splits, dtypes):
    T = x2.shape[0]
    n_out = w.shape[1]
    n_t = wt.shape[0]
    tok = pl.BlockSpec((PROJ_TM, D_MODEL), lambda i: (i, 0))
    return pl.pallas_call(
        functools.partial(_inproj_t_kernel, splits=splits),
        grid=(T // PROJ_TM,),
        in_specs=[tok, _const_spec((1, D_MODEL)), _const_spec((D_MODEL, n_out)), _const_spec((n_t, D_MODEL))],
        out_specs=[pl.BlockSpec((PROJ_TM, c1 - c0), lambda i: (i, 0)) for c0, c1 in splits]
        + [pl.BlockSpec((n_t, PROJ_TM), lambda i: (0, i))],
        out_shape=[jax.ShapeDtypeStruct((T, c1 - c0), dt) for (c0, c1), dt in zip(splits, dtypes)]
        + [jax.ShapeDtypeStruct((n_t, T), F32)],
        compiler_params=_params(1),
        name="inproj_t",
    )(x2, g.reshape(1, D_MODEL), w, wt)


def _gdn_kernel(q_ref, k_ref, v_ref, gate_ref, ab_ref, alog_ref, dtb_ref, cwq_ref, cwk_ref, cwv_ref, ng_ref,
                o_ref,
                xpad_ref, qn_ref, kn_ref, vv_ref, cf_ref, gcrow_ref, g_ref, bm_ref, s16_ref, qt_ref, qu_ref,
                *, seq):
    pad = SUBLANES
    n_blk = seq // GDN_BLOCK
    C = GDN_CHUNK
    RB = 256

    zeros = jnp.zeros((pad, DK), F32)
    xpad_ref[0:pad, :] = zeros
    xpad_ref[pad + seq:pad + seq + pad, :] = zeros

    def conv_silu(src_ref, cw_ref, dst_ref, normalize, scale):
        xpad_ref[pad:pad + seq, :] = src_ref[0]
        cw = cw_ref[...]
        for rb in range(seq // RB):
            r0 = rb * RB
            acc = xpad_ref[pl.ds(pad + r0 - CONV_LEFT, RB), :] * cw[0:1, :]
            for k in range(1, CONV_W):
                acc = acc + xpad_ref[pl.ds(pad + r0 + k - CONV_LEFT, RB), :] * cw[k:k + 1, :]
            y = _silu(acc)
            if normalize:
                y = y * (lax.rsqrt(jnp.sum(y * y, axis=-1, keepdims=True) + 1e-6) * scale)
            dst_ref[r0:r0 + RB, :] = y

    conv_silu(q_ref, cwq_ref, qn_ref, True, DK ** -0.5)
    conv_silu(k_ref, cwk_ref, kn_ref, True, 1.0)
    conv_silu(v_ref, cwv_ref, vv_ref, False, 1.0)

    ab = ab_ref[0]
    g8 = -jnp.exp(alog_ref[0][:, 0:1]) * _softplus(ab + dtb_ref[0][:, 0:1])
    beta8 = _sigmoid(ab)
    lane = lax.broadcasted_iota(jnp.int32, (SUBLANES, seq), 1) % C
    pre = g8
    suf = g8
    sh = 1
    while sh < C:
        pre = pre + jnp.where(lane >= sh, pltpu.roll(pre, sh, axis=1), 0.0)
        suf = suf + jnp.where(lane < C - sh, pltpu.roll(suf, seq - sh, axis=1), 0.0)
        sh *= 2
    tot = pre + suf - g8
    row8 = lax.broadcasted_iota(jnp.int32, (SUBLANES, seq), 0)
    gcs = jnp.where(row8 == 0, pre, suf)
    eg = jnp.exp(gcs)
    ekd = jnp.exp(tot - gcs)
    etot = jnp.exp(tot)
    fill = jnp.zeros((GDN_BLOCK - 5 * SUBLANES, GDN_BLOCK), F32)
    for b in range(n_blk):
        ls = slice(b * GDN_BLOCK, (b + 1) * GDN_BLOCK)
        table = jnp.concatenate([beta8[:, ls], gcs[:, ls], eg[:, ls], ekd[:, ls], etot[:, ls], fill], axis=0)
        cf_ref[b * GDN_BLOCK:(b + 1) * GDN_BLOCK, :] = table.T
        gcrow_ref[b] = gcs[:, ls]

    ri = lax.broadcasted_iota(jnp.int32, (GDN_BLOCK, GDN_BLOCK), 0)
    ci = lax.broadcasted_iota(jnp.int32, (GDN_BLOCK, GDN_BLOCK), 1)
    same = (ri // C) == (ci // C)
    masks = (same & (ri >= ci), same & (ri <= ci))
    stricts = (same & (ri > ci), same & (ri < ci))
    eye = (ri == ci).astype(F32)

    in_chunk = tuple((ci // C) == c for c in range(2))
    sub_sizes = [GDN_INV_BASE]
    while sub_sizes[-1] < C:
        sub_sizes.append(2 * sub_sizes[-1])
    sub_blk = tuple((ri // s) == (ci // s) for s in sub_sizes)

    def local_step(i, carry):
        chains = []
        for bb in range(GDN_LOCAL_UNROLL):
            b = i * GDN_LOCAL_UNROLL + bb
            r0 = pl.multiple_of(b * GDN_BLOCK, GDN_BLOCK)
            kn = kn_ref[pl.ds(r0, GDN_BLOCK), :]
            qn = qn_ref[pl.ds(r0, GDN_BLOCK), :]
            kq = _dot_nt(jnp.concatenate([kn, qn], axis=0).astype(BF16), kn.astype(BF16))
            for d in range(2):
                chains.append(dict(b=b, r0=r0, d=d, kk=kq[:GDN_BLOCK], qk=kq[GDN_BLOCK:]))
        for ch in chains:
            d = ch["d"]
            cf = cf_ref[pl.ds(ch["r0"], GDN_BLOCK), :]
            gcr = gcrow_ref[ch["b"]]
            beta_c = cf[:, ROW_BETA + d:ROW_BETA + d + 1]
            gc_c = cf[:, ROW_GC + d:ROW_GC + d + 1]
            decay = jnp.where(masks[d], jnp.exp(jnp.where(masks[d], gc_c - gcr[d:d + 1, :], 0.0)), 0.0)
            n_mat = jnp.where(stricts[d], -(beta_c * ch["kk"] * decay), 0.0)
            ch["qkd"] = jnp.where(masks[d], ch["qk"] * decay, 0.0).astype(BF16)
            ch["n"] = n_mat
            n0 = jnp.where(sub_blk[0], n_mat, 0.0)
            ch["t"] = eye + n0
            ch["p16"] = n0.astype(BF16)
        for ch in chains:
            ch["p16"] = _dot(ch["p16"], ch["p16"]).astype(BF16)
        for ch in chains:
            pt = _dot(ch["p16"], jnp.concatenate([ch["p16"], ch["t"].astype(BF16)], axis=1))
            ch["p16"] = pt[:, :GDN_BLOCK].astype(BF16)
            ch["t"] = ch["t"] + pt[:, GDN_BLOCK:]
        for ch in chains:
            ch["t"] = ch["t"] + _dot(ch["p16"], ch["t"].astype(BF16))
        for lvl in range(1, len(sub_blk)):
            for ch in chains:
                e16 = jnp.where(sub_blk[lvl] & ~sub_blk[lvl - 1], ch["n"], 0.0).astype(BF16)
                ch["et"] = _dot(e16, ch["t"].astype(BF16)).astype(BF16)
            for ch in chains:
                ch["t"] = ch["t"] + _dot(ch["t"].astype(BF16), ch["et"])
        for ch in chains:
            d = ch["d"]
            rows = pl.ds(ch["r0"], GDN_BLOCK)
            cf = cf_ref[rows, :]
            kn = kn_ref[rows, :]
            beta_c = cf[:, ROW_BETA + d:ROW_BETA + d + 1]
            eg_c = cf[:, ROW_EG + d:ROW_EG + d + 1]
            rhs = jnp.concatenate([kn * (beta_c * eg_c), vv_ref[rows, :] * beta_c], axis=1).astype(BF16)
            ch["wu"] = _dot(ch["t"].astype(BF16), rhs).astype(BF16)
        for ch in chains:
            d, b = ch["d"], ch["b"]
            rows = pl.ds(ch["r0"], GDN_BLOCK)
            cf = cf_ref[rows, :]
            eg_c = cf[:, ROW_EG + d:ROW_EG + d + 1]
            ekd_c = cf[:, ROW_EKD + d:ROW_EKD + d + 1]
            kdt = (kn_ref[rows, :] * ekd_c).T
            for c in range(2):
                gb = _dot(jnp.where(in_chunk[c], kdt, 0.0).astype(BF16), ch["wu"])
                g_ref[d, 2 * b + c] = (-gb[:, :DK]).astype(BF16)
                bm_ref[d, 2 * b + c] = gb[:, DK:]
            qwu = _dot(ch["qkd"], ch["wu"])
            qt_ref[d, rows, :] = (qn_ref[rows, :] * eg_c - qwu[:, :DK]).astype(BF16)
            qu_ref[d, rows, :] = qwu[:, DK:]
        return carry

    lax.fori_loop(0, n_blk // GDN_LOCAL_UNROLL, local_step, 0)

    n_chunks = seq // C

    def advance(d, n, state):
        s16 = state.astype(BF16)
        s16_ref[d, n] = s16
        et = cf_ref[pl.ds(pl.multiple_of(n * C, C), 1), ROW_ETOT + d:ROW_ETOT + d + 1]
        return state * et + (_dot(g_ref[d, n], s16) + bm_ref[d, n])

    def state_step(n, carry):
        sf, sb = carry
        return advance(0, n, sf), advance(1, n_chunks - 1 - n, sb)

    s0 = jnp.zeros((DK, DV), F32)
    lax.fori_loop(0, n_chunks, state_step, (s0, s0), unroll=2)

    ng = ng_ref[...]

    def out_step(b, carry):
        r0 = pl.multiple_of(b * GDN_BLOCK, GDN_BLOCK)
        halves = []
        for c in range(2):
            rows = pl.ds(r0 + c * C, C)
            o = qu_ref[0, rows, :] + qu_ref[1, rows, :]
            for d in range(2):
                o = o + _dot(qt_ref[d, rows, :], s16_ref[d, 2 * b + c])
            halves.append(o)
        o = jnp.concatenate(halves, axis=0)
        gate = gate_ref[0, pl.ds(r0, GDN_BLOCK), :].astype(F32)
        o_ref[0, pl.ds(r0, GDN_BLOCK), :] = (_rms(o, ng) * _silu(gate)).astype(o_ref.dtype)
        return carry

    lax.fori_loop(0, n_blk, out_step, 0, unroll=2)


def _gdn_core(qkv3, gate3, abt3, alog8, dtb8, conv_w, norm_g):
    B, S, _ = qkv3.shape
    n_blk = S // GDN_BLOCK
    n_chunks = S // GDN_CHUNK

    def col(off):
        return pl.BlockSpec((1, S, DK), lambda b, h: (b, 0, off + h))

    def cw(off):
        return pl.BlockSpec((CONV_W, DK), lambda b, h: (0, off + h))

    par = pl.BlockSpec((1, SUBLANES, LANES), lambda b, h: (h, 0, 0))
    return pl.pallas_call(
        functools.partial(_gdn_kernel, seq=S),
        grid=(B, HC),
        in_specs=[col(0), col(HC), col(2 * HC),
                  pl.BlockSpec((1, S, DV), lambda b, h: (b, 0, h)),
                  pl.BlockSpec((1, SUBLANES, S), lambda b, h: (h, 0, b)),
                  par, par, cw(0), cw(HC), cw(2 * HC),
                  pl.BlockSpec((1, DV), lambda b, h: (0, 0))],
        out_specs=pl.BlockSpec((1, S, DV), lambda b, h: (b, 0, h)),
        out_shape=jax.ShapeDtypeStruct((B, S, DC), BF16),
        scratch_shapes=[pltpu.VMEM((S + 2 * SUBLANES, DK), F32),
                        pltpu.VMEM((S, DK), F32), pltpu.VMEM((S, DK), F32), pltpu.VMEM((S, DV), F32),
                        pltpu.VMEM((S, LANES), F32), pltpu.VMEM((n_blk, SUBLANES, GDN_BLOCK), F32),
                        pltpu.VMEM((2, n_chunks, DK, DK), BF16), pltpu.VMEM((2, n_chunks, DK, DV), F32),
                        pltpu.VMEM((2, n_chunks, DK, DV), BF16),
                        pltpu.VMEM((2, S, DK), BF16), pltpu.VMEM((2, S, DV), F32)],
        compiler_params=_params(2),
        name="gdn",
    )(qkv3, qkv3, qkv3, gate3, abt3, alog8, dtb8, conv_w, conv_w, conv_w, norm_g.reshape(1, DV))


def _mixer_gdn(x2, B, S, g, w_in, conv_w, a_log, dt_bias, norm_g, w_out):
    w_ab = w_in[:, 4 * DC:].reshape(D_MODEL, 2, 2, HC).transpose(3, 1, 2, 0).reshape(HC, 4, D_MODEL)
    w_ab = jnp.concatenate([w_ab, jnp.zeros_like(w_ab)], axis=1).reshape(HC * SUBLANES, D_MODEL)
    qkv, gate, abt = _inproj_t(x2, g, w_in[:, :4 * DC].astype(BF16), w_ab.astype(BF16),
                               ((0, 3 * DC), (3 * DC, 4 * DC)), (F32, BF16))

    def per_head(p):
        p8 = jnp.concatenate([p.T.astype(F32), jnp.zeros((HC, SUBLANES - 2), F32)], axis=1)
        return jnp.broadcast_to(p8[:, :, None], (HC, SUBLANES, LANES))

    y = _gdn_core(qkv.reshape(B, S, 3 * DC), gate.reshape(B, S, DC), abt.reshape(HC, SUBLANES, B * S),
                  per_head(a_log), per_head(dt_bias), conv_w, norm_g)
    return _outproj(x2, y.reshape(B * S, DC), w_out.astype(BF16))


def kernel(x, norm_mix_g, norm_mlp_g, mlp_w_up, mlp_w_down, norm_final_g, a_w_in, a_conv_w, a_conv_b, a_gate_w, a_gate_b, a_lambda, a_w_out, b_w_in, b_ln_g, b_ln_b, b_w_s, b_b_s, b_w_out, c_w_in, c_conv_w, c_a_log, c_dt_bias, c_norm_g, c_w_out):
    B, S, _ = x.shape
    depth = norm_mix_g.shape[0]
    x2 = x.reshape(B * S, D_MODEL)
    for i in range(depth):
        kind, j = i % N_MIXERS, i // N_MIXERS
        g = norm_mix_g[i]
        if kind == 0:
            x2 = _mixer_rglru(x2, B, S, g, a_w_in[j], a_conv_w[j], a_conv_b[j], a_gate_w[j], a_gate_b[j],
                              a_lambda[j], a_w_out[j])
        elif kind == 1:
            x2 = _sgu(x2, g, b_w_in[j], b_ln_g[j], b_ln_b[j], b_w_s[j], b_b_s[j], b_w_out[j])
        else:
            x2 = _mixer_gdn(x2, B, S, g, c_w_in[j], c_conv_w[j], c_a_log[j], c_dt_bias[j], c_norm_g[j], c_w_out[j])
        x2 = _mlp(x2, norm_mlp_g[i], mlp_w_up[i].astype(BF16), mlp_w_down[i].astype(BF16),
                  norm_final_g if i == depth - 1 else None)
    return x2.reshape(B, S, D_MODEL)
```

```python
import functools

import jax
import jax.numpy as jnp
from jax import lax
from jax.experimental import pallas as pl
from jax.experimental.pallas import tpu as pltpu

F32 = jnp.float32
BF16 = jnp.bfloat16

D_MODEL = 1024
D_FF = 4 * D_MODEL
N_MIXERS = 3
LANES = 128
SUBLANES = 8
VMEM_LIMIT = 56 * 1024 * 1024

HA = 8
HDA = D_MODEL // HA
CONV_W = 4
CONV_LEFT = CONV_W // 2
RG_C = 8.0
LOG2E = 1.4426950408889634
RG_SEG = 64
SEG_PAD = 8
RG_ROWS = 256
SGU_CHUNK = 128
GB = 8
DGB = D_MODEL // GB
HC = 8
DK = 128
DV = 128
DC = HC * DV
GDN_CHUNK = 64
GDN_BLOCK = 2 * GDN_CHUNK
GDN_INV_BASE = 8
GDN_LOCAL_UNROLL = 8


def _params(n_parallel):
    return pltpu.CompilerParams(dimension_semantics=("parallel",) * n_parallel,
                                vmem_limit_bytes=VMEM_LIMIT)


def _rms(xf, g, eps=1e-6):
    return xf * lax.rsqrt(jnp.mean(xf * xf, axis=-1, keepdims=True) + eps) * g


def _sigmoid(x):
    return 0.5 * jnp.tanh(0.5 * x) + 0.5


def _silu(x):
    return x * _sigmoid(x)


def _gelu_tanh(x):
    c = 0.7978845608028654
    return 0.5 * x * (1.0 + jnp.tanh(c * (x + 0.044715 * (x * x * x))))


def _softplus(x):
    return jnp.maximum(x, 0.0) + jnp.log1p(jnp.exp(-jnp.abs(x)))


def _dot(a, b):
    return jnp.dot(a, b, preferred_element_type=F32)


def _dot_nt(a, b):
    return lax.dot_general(a, b, (((1,), (1,)), ((), ())), preferred_element_type=F32)


MLP_TM = 512
MLP_FC = 1024


def _mlp_kernel(x_ref, g_ref, wup_ref, wdn_ref, *rest, final_norm, mixer_proj):
    rest = list(rest)
    o_ref = rest.pop()
    x = x_ref[...]
    if mixer_proj:
        y_ref, wo_ref = rest[:2]
        rest = rest[2:]
        x = x + _dot(y_ref[...], wo_ref[...])
    if final_norm:
        (gf_ref,) = rest
    h = _rms(x, g_ref[...]).astype(BF16)
    acc = x
    for c in range(D_FF // MLP_FC):
        u = _dot(h, wup_ref[:, c * MLP_FC:(c + 1) * MLP_FC])
        a = jnp.square(jnp.maximum(u, 0.0)).astype(BF16)
        acc = acc + _dot(a, wdn_ref[c * MLP_FC:(c + 1) * MLP_FC, :])
    if final_norm:
        acc = _rms(acc, gf_ref[...])
    o_ref[...] = acc


def _const_spec(shape):
    n = len(shape)
    return pl.BlockSpec(shape, lambda *_: (0,) * n, pipeline_mode=pl.Buffered(1))


def _mlp(x2, g, w_up, w_down, g_final=None, mixer_y=None, mixer_w_out=None):
    T = x2.shape[0]
    tok = pl.BlockSpec((MLP_TM, D_MODEL), lambda i: (i, 0))
    in_specs = [tok, _const_spec((1, D_MODEL)), _const_spec((D_MODEL, D_FF)), _const_spec((D_FF, D_MODEL))]
    args = [x2, g.reshape(1, D_MODEL), w_up, w_down]
    if mixer_y is not None:
        in_specs += [tok, _const_spec((D_MODEL, D_MODEL))]
        args += [mixer_y, mixer_w_out]
    if g_final is not None:
        in_specs.append(_const_spec((1, D_MODEL)))
        args.append(g_final.reshape(1, D_MODEL))
    return pl.pallas_call(
        functools.partial(_mlp_kernel, final_norm=g_final is not None, mixer_proj=mixer_y is not None),
        grid=(T // MLP_TM,),
        in_specs=in_specs,
        out_specs=tok,
        out_shape=jax.ShapeDtypeStruct((T, D_MODEL), F32),
        compiler_params=_params(1),
        name="mlp",
    )(*args)


PROJ_TM = 512


def _inproj_kernel(x_ref, g_ref, w_ref, *o_refs, splits):
    h = _rms(x_ref[...], g_ref[...]).astype(BF16)
    for o_ref, (c0, c1) in zip(o_refs, splits):
        o_ref[...] = _dot(h, w_ref[:, c0:c1]).astype(o_ref.dtype)


def _inproj(x2, g, w, splits, dtypes):
    T = x2.shape[0]
    n_out = w.shape[1]
    tok = pl.BlockSpec((PROJ_TM, D_MODEL), lambda i: (i, 0))
    return pl.pallas_call(
        functools.partial(_inproj_kernel, splits=splits),
        grid=(T // PROJ_TM,),
        in_specs=[tok, _const_spec((1, D_MODEL)), _const_spec((D_MODEL, n_out))],
        out_specs=[pl.BlockSpec((PROJ_TM, c1 - c0), lambda i: (i, 0)) for c0, c1 in splits],
        out_shape=[jax.ShapeDtypeStruct((T, c1 - c0), dt) for (c0, c1), dt in zip(splits, dtypes)],
        compiler_params=_params(1),
        name="inproj",
    )(x2, g.reshape(1, D_MODEL), w)


SGU_TM = 512


def _sgu_kernel(x_ref, g_ref, win_ref, lng_ref, lnb_ref, ws_ref, bs_ref, wout_ref, o_ref, y_ref):
    x = x_ref[...]
    h = _rms(x, g_ref[...]).astype(BF16)
    u = _gelu_tanh(_dot(h, win_ref[:, :D_MODEL]))
    v = _gelu_tanh(_dot(h, win_ref[:, D_MODEL:]))
    mu = jnp.mean(v, axis=-1, keepdims=True)
    vc = v - mu
    var = jnp.mean(vc * vc, axis=-1, keepdims=True)
    v = (vc * lax.rsqrt(var + 1e-5) * lng_ref[...] + lnb_ref[...]).astype(BF16)
    n_chunks = SGU_TM // SGU_CHUNK
    for gi in range(GB):
        cols = slice(gi * DGB, (gi + 1) * DGB)
        vg = jnp.concatenate([v[c * SGU_CHUNK:(c + 1) * SGU_CHUNK, cols] for c in range(n_chunks)], axis=1)
        vs = _dot(ws_ref[gi], vg)
        for c in range(n_chunks):
            rows = slice(c * SGU_CHUNK, (c + 1) * SGU_CHUNK)
            vsc = vs[:, c * DGB:(c + 1) * DGB] + bs_ref[gi]
            y_ref[rows, cols] = (u[rows, cols] * vsc).astype(BF16)
    o_ref[...] = x + _dot(y_ref[...], wout_ref[...])


def _sgu(x2, g, w_in, ln_g, ln_b, w_s, b_s, w_out):
    T = x2.shape[0]
    tok = pl.BlockSpec((SGU_TM, D_MODEL), lambda i: (i, 0))
    bs = jnp.broadcast_to(b_s[:, :, None], (GB, SGU_CHUNK, DGB)).astype(F32)
    return pl.pallas_call(
        _sgu_kernel,
        grid=(T // SGU_TM,),
        in_specs=[tok, _const_spec((1, D_MODEL)), _const_spec((D_MODEL, 2 * D_MODEL)),
                  _const_spec((1, D_MODEL)), _const_spec((1, D_MODEL)),
                  _const_spec((GB, SGU_CHUNK, SGU_CHUNK)), _const_spec((GB, SGU_CHUNK, DGB)),
                  _const_spec((D_MODEL, D_MODEL))],
        out_specs=tok,
        out_shape=jax.ShapeDtypeStruct((T, D_MODEL), F32),
        scratch_shapes=[pltpu.VMEM((SGU_TM, D_MODEL), BF16)],
        compiler_params=_params(1),
        name="sgu",
    )(x2, g.reshape(1, D_MODEL), w_in.astype(BF16), ln_g.reshape(1, D_MODEL), ln_b.reshape(1, D_MODEL),
      w_s.astype(BF16), bs, w_out.astype(BF16))


def _rglru_kernel(xr_ref, gate_ref, cw_ref, cb_ref, gw_ref, gb_ref, lam_ref, o_ref,
                  xpad_ref, a_ref, b_ref, h_ref, p_ref, *, seq):
    seg = RG_SEG
    pitch = seg + SEG_PAD
    n_seg = seq // seg
    n_grp = n_seg // SUBLANES
    seg_per_blk = RG_ROWS // seg
    pad = SUBLANES

    zeros = jnp.zeros((pad, HDA), F32)
    xpad_ref[0:pad, :] = zeros
    xpad_ref[pad + seq:pad + seq + pad, :] = zeros
    xpad_ref[pad:pad + seq, :] = xr_ref[0]

    cw = cw_ref[...]
    cb = cb_ref[...]
    gw = gw_ref[0]
    gbias = gb_ref[0]
    hn = [(-0.5 * RG_C * LOG2E) * _softplus(-lam_ref[d:d + 1, :]) for d in range(2)]

    for rb in range(seq // RG_ROWS):
        r0 = rb * RG_ROWS
        xc = cb
        for k in range(CONV_W):
            xc = xc + xpad_ref[pl.ds(pad + r0 + k - CONV_LEFT, RG_ROWS), :] * cw[k:k + 1, :]
        gates = jnp.tanh(_dot(xc.astype(BF16), gw) + gbias)
        xh = 0.5 * xc
        for d in range(2):
            t_r = gates[:, (2 * d) * HDA:(2 * d + 1) * HDA]
            t_i = gates[:, (2 * d + 1) * HDA:(2 * d + 2) * HDA]
            a = jnp.exp2(t_r * hn[d] + hn[d])
            y = 1.0 - a * a
            bx = (y * lax.rsqrt(jnp.maximum(y, 1e-30))) * ((t_i + 1.0) * xh)
            for j in range(seg_per_blk):
                q = rb * seg_per_blk + j
                a_ref[d, q * pitch:q * pitch + seg, :] = a[j * seg:(j + 1) * seg]
                b_ref[d, q * pitch:q * pitch + seg, :] = bx[j * seg:(j + 1) * seg]

    def scan_step(i, carry):
        out = []
        j = seg - 1 - i
        for g in range(n_grp):
            hf, pf, hb, pb = carry[4 * g:4 * g + 4]
            base = g * SUBLANES * pitch
            fwd = pl.ds(base + i, SUBLANES, stride=pitch)
            rev = pl.ds(base + j, SUBLANES, stride=pitch)
            af = a_ref[0, fwd, :]
            ab = a_ref[1, rev, :]
            hf = af * hf + b_ref[0, fwd, :]
            pf = af * pf
            hb = ab * hb + b_ref[1, rev, :]
            pb = ab * pb
            h_ref[0, fwd, :] = hf
            p_ref[0, fwd, :] = pf
            h_ref[1, rev, :] = hb
            p_ref[1, rev, :] = pb
            out += [hf, pf, hb, pb]
        return tuple(out)

    z = jnp.zeros((SUBLANES, HDA), F32)
    o = jnp.ones((SUBLANES, HDA), F32)
    ends = lax.fori_loop(0, seg, scan_step, (z, o, z, o) * n_grp, unroll=2)

    row = lax.broadcasted_iota(jnp.int32, (SUBLANES, HDA), 0)
    cf = [None] * n_grp
    cr = [None] * n_grp
    c_in = jnp.zeros((1, HDA), F32)
    for g in range(n_grp):
        hf, pf = ends[4 * g], ends[4 * g + 1]
        c = jnp.broadcast_to(c_in, (SUBLANES, HDA))
        for _ in range(SUBLANES - 1):
            c = jnp.where(row == 0, c_in, pltpu.roll(hf + pf * c, 1, axis=0))
        cf[g] = c
        c_in = (hf + pf * c)[SUBLANES - 1:SUBLANES, :]
    c_in = jnp.zeros((1, HDA), F32)
    for g in reversed(range(n_grp)):
        hb, pb = ends[4 * g + 2], ends[4 * g + 3]
        c = jnp.broadcast_to(c_in, (SUBLANES, HDA))
        for _ in range(SUBLANES - 1):
            c = jnp.where(row == SUBLANES - 1, c_in, pltpu.roll(hb + pb * c, SUBLANES - 1, axis=0))
        cr[g] = c
        c_in = (hb + pb * c)[0:1, :]

    for q in range(n_seg):
        g, s = divmod(q, SUBLANES)
        rows = slice(q * pitch, q * pitch + seg)
        yf = h_ref[0, rows, :] + p_ref[0, rows, :] * cf[g][s:s + 1, :]
        yb = h_ref[1, rows, :] + p_ref[1, rows, :] * cr[g][s:s + 1, :]
        gate = gate_ref[0, q * seg:(q + 1) * seg, :].astype(F32)
        o_ref[0, q * seg:(q + 1) * seg, :] = ((yf + yb) * _gelu_tanh(gate)).astype(o_ref.dtype)


def _rglru_core(gate3, xr3, conv_w, conv_b, gate_w, gate_b, lam):
    B, S, _ = xr3.shape
    rows = (S // RG_SEG) * (RG_SEG + SEG_PAD)
    blk = pl.BlockSpec((1, S, HDA), lambda b, h: (b, 0, h))
    gw = (0.5 * gate_w).transpose(2, 3, 0, 1, 4).reshape(HA, HDA, 4 * HDA).astype(BF16)
    gb = (0.5 * gate_b).transpose(2, 0, 1, 3).reshape(HA, 1, 4 * HDA)
    return pl.pallas_call(
        functools.partial(_rglru_kernel, seq=S),
        grid=(B, HA),
        in_specs=[blk, blk,
                  pl.BlockSpec((CONV_W, HDA), lambda b, h: (0, h)),
                  pl.BlockSpec((1, HDA), lambda b, h: (0, h)),
                  pl.BlockSpec((1, HDA, 4 * HDA), lambda b, h: (h, 0, 0)),
                  pl.BlockSpec((1, 1, 4 * HDA), lambda b, h: (h, 0, 0)),
                  pl.BlockSpec((2, HDA), lambda b, h: (0, h))],
        out_specs=blk,
        out_shape=jax.ShapeDtypeStruct((B, S, D_MODEL), BF16),
        scratch_shapes=[pltpu.VMEM((S + 2 * SUBLANES, HDA), F32),
                        pltpu.VMEM((2, rows, HDA), F32), pltpu.VMEM((2, rows, HDA), F32),
                        pltpu.VMEM((2, rows, HDA), F32), pltpu.VMEM((2, rows, HDA), F32)],
        compiler_params=_params(2),
        name="rglru",
    )(xr3, gate3, conv_w, conv_b.reshape(1, D_MODEL), gw, gb, lam)


def _mixer_rglru(x2, B, S, g, w_in, conv_w, conv_b, gate_w, gate_b, lam):
    gate, xr = _inproj(x2, g, w_in.astype(BF16), ((0, D_MODEL), (D_MODEL, 2 * D_MODEL)), (BF16, F32))
    y = _rglru_core(gate.reshape(B, S, D_MODEL), xr.reshape(B, S, D_MODEL),
                    conv_w, conv_b, gate_w, gate_b, lam)
    return y.reshape(B * S, D_MODEL)


ROW_BETA, ROW_GC, ROW_EG, ROW_EKD, ROW_ETOT = 2, 8, 16, 24, 32


def _inproj_t_kernel(x_ref, g_ref, w_ref, wt_ref, *o_refs, splits):
    h = _rms(x_ref[...], g_ref[...]).astype(BF16)
    for o_ref, (c0, c1) in zip(o_refs[:-1], splits):
        o_ref[...] = _dot(h, w_ref[:, c0:c1]).astype(o_ref.dtype)
    o_refs[-1][...] = _dot_nt(wt_ref[...], h)


def _inproj_t(x2, g, w, wt, splits, dtypes):
    T = x2.shape[0]
    n_out = w.shape[1]
    n_t = wt.shape[0]
    tok = pl.BlockSpec((PROJ_TM, D_MODEL), lambda i: (i, 0))
    return pl.pallas_call(
        functools.partial(_inproj_t_kernel, splits=splits),
        grid=(T // PROJ_TM,),
        in_specs=[tok, _const_spec((1, D_MODEL)), _const_spec((D_MODEL, n_out)), _const_spec((n_t, D_MODEL))],
        out_specs=[pl.BlockSpec((PROJ_TM, c1 - c0), lambda i: (i, 0)) for c0, c1 in splits]
        + [pl.BlockSpec((n_t, PROJ_TM), lambda i: (0, i))],
        out_shape=[jax.ShapeDtypeStruct((T, c1 - c0), dt) for (c0, c1), dt in zip(splits, dtypes)]
        + [jax.ShapeDtypeStruct((n_t, T), F32)],
        compiler_params=_params(1),
        name="inproj_t",
    )(x2, g.reshape(1, D_MODEL), w, wt)


def _gdn_kernel(q_ref, k_ref, v_ref, gate_ref, ab_ref, alog_ref, dtb_ref, cwq_ref, cwk_ref, cwv_ref, ng_ref,
                o_ref,
                xpad_ref, qn_ref, kn_ref, vv_ref, cf_ref, gcrow_ref, g_ref, bm_ref, qt_ref, qu_ref,
                *, seq):
    pad = SUBLANES
    n_blk = seq // GDN_BLOCK
    C = GDN_CHUNK
    RB = 256

    zeros = jnp.zeros((pad, DK), F32)
    xpad_ref[0:pad, :] = zeros
    xpad_ref[pad + seq:pad + seq + pad, :] = zeros

    def conv_silu(src_ref, cw_ref, dst_ref, normalize, scale):
        xpad_ref[pad:pad + seq, :] = src_ref[0]
        cw = cw_ref[...]
        for rb in range(seq // RB):
            r0 = rb * RB
            acc = xpad_ref[pl.ds(pad + r0 - CONV_LEFT, RB), :] * cw[0:1, :]
            for k in range(1, CONV_W):
                acc = acc + xpad_ref[pl.ds(pad + r0 + k - CONV_LEFT, RB), :] * cw[k:k + 1, :]
            y = _silu(acc)
            if normalize:
                y = y * (lax.rsqrt(jnp.sum(y * y, axis=-1, keepdims=True) + 1e-6) * scale)
            dst_ref[r0:r0 + RB, :] = y

    conv_silu(q_ref, cwq_ref, qn_ref, True, DK ** -0.5)
    conv_silu(k_ref, cwk_ref, kn_ref, True, 1.0)
    conv_silu(v_ref, cwv_ref, vv_ref, False, 1.0)

    ab = ab_ref[0]
    g8 = -jnp.exp(alog_ref[0][:, 0:1]) * _softplus(ab + dtb_ref[0][:, 0:1])
    beta8 = _sigmoid(ab)
    lane = lax.broadcasted_iota(jnp.int32, (SUBLANES, seq), 1) % C
    pre = g8
    suf = g8
    sh = 1
    while sh < C:
        pre = pre + jnp.where(lane >= sh, pltpu.roll(pre, sh, axis=1), 0.0)
        suf = suf + jnp.where(lane < C - sh, pltpu.roll(suf, seq - sh, axis=1), 0.0)
        sh *= 2
    tot = pre + suf - g8
    row8 = lax.broadcasted_iota(jnp.int32, (SUBLANES, seq), 0)
    gcs = jnp.where(row8 == 0, pre, suf)
    eg = jnp.exp(gcs)
    ekd = jnp.exp(tot - gcs)
    etot = jnp.exp(tot)
    fill = jnp.zeros((GDN_BLOCK - 5 * SUBLANES, GDN_BLOCK), F32)
    for b in range(n_blk):
        ls = slice(b * GDN_BLOCK, (b + 1) * GDN_BLOCK)
        table = jnp.concatenate([beta8[:, ls], gcs[:, ls], eg[:, ls], ekd[:, ls], etot[:, ls], fill], axis=0)
        cf_ref[b * GDN_BLOCK:(b + 1) * GDN_BLOCK, :] = table.T
        gcrow_ref[b] = gcs[:, ls]

    ri = lax.broadcasted_iota(jnp.int32, (GDN_BLOCK, GDN_BLOCK), 0)
    ci = lax.broadcasted_iota(jnp.int32, (GDN_BLOCK, GDN_BLOCK), 1)
    same = (ri // C) == (ci // C)
    masks = (same & (ri >= ci), same & (ri <= ci))
    stricts = (same & (ri > ci), same & (ri < ci))
    eye = (ri == ci).astype(F32)

    in_chunk = tuple((ci // C) == c for c in range(2))
    sub_sizes = [GDN_INV_BASE]
    while sub_sizes[-1] < C:
        sub_sizes.append(2 * sub_sizes[-1])
    sub_blk = tuple((ri // s) == (ci // s) for s in sub_sizes)

    def local_step(i, carry):
        chains = []
        for bb in range(GDN_LOCAL_UNROLL):
            b = i * GDN_LOCAL_UNROLL + bb
            r0 = pl.multiple_of(b * GDN_BLOCK, GDN_BLOCK)
            kn = kn_ref[pl.ds(r0, GDN_BLOCK), :]
            qn = qn_ref[pl.ds(r0, GDN_BLOCK), :]
            kq = _dot_nt(jnp.concatenate([kn, qn], axis=0).astype(BF16), kn.astype(BF16))
            for d in range(2):
                chains.append(dict(b=b, r0=r0, d=d, kk=kq[:GDN_BLOCK], qk=kq[GDN_BLOCK:]))
        for ch in chains:
            d = ch["d"]
            cf = cf_ref[pl.ds(ch["r0"], GDN_BLOCK), :]
            gcr = gcrow_ref[ch["b"]]
            beta_c = cf[:, ROW_BETA + d:ROW_BETA + d + 1]
            gc_c = cf[:, ROW_GC + d:ROW_GC + d + 1]
            decay = jnp.where(masks[d], jnp.exp(jnp.where(masks[d], gc_c - gcr[d:d + 1, :], 0.0)), 0.0)
            n_mat = jnp.where(stricts[d], -(beta_c * ch["kk"] * decay), 0.0)
            ch["qkd"] = jnp.where(masks[d], ch["qk"] * decay, 0.0).astype(BF16)
            ch["n"] = n_mat
            n0 = jnp.where(sub_blk[0], n_mat, 0.0)
            ch["t"] = eye + n0
            ch["p16"] = n0.astype(BF16)
        for ch in chains:
            ch["p16"] = _dot(ch["p16"], ch["p16"]).astype(BF16)
        for ch in chains:
            pt = _dot(ch["p16"], jnp.concatenate([ch["p16"], ch["t"].astype(BF16)], axis=1))
            ch["p16"] = pt[:, :GDN_BLOCK].astype(BF16)
            ch["t"] = ch["t"] + pt[:, GDN_BLOCK:]
        for ch in chains:
            ch["t"] = ch["t"] + _dot(ch["p16"], ch["t"].astype(BF16))
        for lvl in range(1, len(sub_blk)):
            for ch in chains:
                e16 = jnp.where(sub_blk[lvl] & ~sub_blk[lvl - 1], ch["n"], 0.0).astype(BF16)
                ch["et"] = _dot(e16, ch["t"].astype(BF16)).astype(BF16)
            for ch in chains:
                ch["t"] = ch["t"] + _dot(ch["t"].astype(BF16), ch["et"])
        for ch in chains:
            d = ch["d"]
            rows = pl.ds(ch["r0"], GDN_BLOCK)
            cf = cf_ref[rows, :]
            kn = kn_ref[rows, :]
            beta_c = cf[:, ROW_BETA + d:ROW_BETA + d + 1]
            eg_c = cf[:, ROW_EG + d:ROW_EG + d + 1]
            rhs = jnp.concatenate([kn * (beta_c * eg_c), vv_ref[rows, :] * beta_c], axis=1).astype(BF16)
            ch["wu"] = _dot(ch["t"].astype(BF16), rhs).astype(BF16)
        for ch in chains:
            d, b = ch["d"], ch["b"]
            rows = pl.ds(ch["r0"], GDN_BLOCK)
            cf = cf_ref[rows, :]
            eg_c = cf[:, ROW_EG + d:ROW_EG + d + 1]
            ekd_c = cf[:, ROW_EKD + d:ROW_EKD + d + 1]
            kdt = (kn_ref[rows, :] * ekd_c).T
            for c in range(2):
                gb = _dot(jnp.where(in_chunk[c], kdt, 0.0).astype(BF16), ch["wu"])
                g_ref[d, 2 * b + c] = (-gb[:, :DK]).astype(BF16)
                bm_ref[d, 2 * b + c] = gb[:, DK:]
            qwu = _dot(ch["qkd"], ch["wu"])
            qt_ref[d, rows, :] = (qn_ref[rows, :] * eg_c - qwu[:, :DK]).astype(BF16)
            qu_ref[d, rows, :] = qwu[:, DK:]
        return carry

    lax.fori_loop(0, n_blk // GDN_LOCAL_UNROLL, local_step, 0)

    n_chunks = seq // C

    def state_step(n, carry):
        states = list(carry)
        chunks = (n, n_chunks - 1 - n)
        s16 = [s.astype(BF16) for s in states]
        for d in range(2):
            r0 = pl.multiple_of(chunks[d] * C, C)
            et = cf_ref[pl.ds(r0, 1), ROW_ETOT + d:ROW_ETOT + d + 1]
            states[d] = states[d] * et + (_dot(g_ref[d, chunks[d]], s16[d]) + bm_ref[d, chunks[d]])
        for d in range(2):
            rows = pl.ds(pl.multiple_of(chunks[d] * C, C), C)
            qu_ref[d, rows, :] = qu_ref[d, rows, :] + _dot(qt_ref[d, rows, :], s16[d])
        return tuple(states)

    s0 = jnp.zeros((DK, DV), F32)
    lax.fori_loop(0, n_chunks, state_step, (s0, s0), unroll=2)

    ng = ng_ref[...]
    for rb in range(seq // RB):
        rows = slice(rb * RB, (rb + 1) * RB)
        o = qu_ref[0, rows, :] + qu_ref[1, rows, :]
        gate = gate_ref[0, rows, :].astype(F32)
        o_ref[0, rows, :] = (_rms(o, ng) * _silu(gate)).astype(o_ref.dtype)


def _gdn_core(qkv3, gate3, abt3, alog8, dtb8, conv_w, norm_g):
    B, S, _ = qkv3.shape
    n_blk = S // GDN_BLOCK
    n_chunks = S // GDN_CHUNK

    def col(off):
        return pl.BlockSpec((1, S, DK), lambda b, h: (b, 0, off + h))

    def cw(off):
        return pl.BlockSpec((CONV_W, DK), lambda b, h: (0, off + h))

    par = pl.BlockSpec((1, SUBLANES, LANES), lambda b, h: (h, 0, 0))
    return pl.pallas_call(
        functools.partial(_gdn_kernel, seq=S),
        grid=(B, HC),
        in_specs=[col(0), col(HC), col(2 * HC),
                  pl.BlockSpec((1, S, DV), lambda b, h: (b, 0, h)),
                  pl.BlockSpec((1, SUBLANES, S), lambda b, h: (h, 0, b)),
                  par, par, cw(0), cw(HC), cw(2 * HC),
                  pl.BlockSpec((1, DV), lambda b, h: (0, 0))],
        out_specs=pl.BlockSpec((1, S, DV), lambda b, h: (b, 0, h)),
        out_shape=jax.ShapeDtypeStruct((B, S, DC), BF16),
        scratch_shapes=[pltpu.VMEM((S + 2 * SUBLANES, DK), F32),
                        pltpu.VMEM((S, DK), F32), pltpu.VMEM((S, DK), F32), pltpu.VMEM((S, DV), F32),
                        pltpu.VMEM((S, LANES), F32), pltpu.VMEM((n_blk, SUBLANES, GDN_BLOCK), F32),
                        pltpu.VMEM((2, n_chunks, DK, DK), BF16), pltpu.VMEM((2, n_chunks, DK, DV), F32),
                        pltpu.VMEM((2, S, DK), BF16), pltpu.VMEM((2, S, DV), F32)],
        compiler_params=_params(2),
        name="gdn",
    )(qkv3, qkv3, qkv3, gate3, abt3, alog8, dtb8, conv_w, conv_w, conv_w, norm_g.reshape(1, DV))


def _mixer_gdn(x2, B, S, g, w_in, conv_w, a_log, dt_bias, norm_g):
    w_ab = w_in[:, 4 * DC:].reshape(D_MODEL, 2, 2, HC).transpose(3, 1, 2, 0).reshape(HC, 4, D_MODEL)
    w_ab = jnp.concatenate([w_ab, jnp.zeros_like(w_ab)], axis=1).reshape(HC * SUBLANES, D_MODEL)
    qkv, gate, abt = _inproj_t(x2, g, w_in[:, :4 * DC].astype(BF16), w_ab.astype(BF16),
                               ((0, 3 * DC), (3 * DC, 4 * DC)), (F32, BF16))

    def per_head(p):
        p8 = jnp.concatenate([p.T.astype(F32), jnp.zeros((HC, SUBLANES - 2), F32)], axis=1)
        return jnp.broadcast_to(p8[:, :, None], (HC, SUBLANES, LANES))

    y = _gdn_core(qkv.reshape(B, S, 3 * DC), gate.reshape(B, S, DC), abt.reshape(HC, SUBLANES, B * S),
                  per_head(a_log), per_head(dt_bias), conv_w, norm_g)
    return y.reshape(B * S, DC)


def kernel(x, norm_mix_g, norm_mlp_g, mlp_w_up, mlp_w_down, norm_final_g, a_w_in, a_conv_w, a_conv_b, a_gate_w, a_gate_b, a_lambda, a_w_out, b_w_in, b_ln_g, b_ln_b, b_w_s, b_b_s, b_w_out, c_w_in, c_conv_w, c_a_log, c_dt_bias, c_norm_g, c_w_out):
    B, S, _ = x.shape
    depth = norm_mix_g.shape[0]
    x2 = x.reshape(B * S, D_MODEL)
    for i in range(depth):
        kind, j = i % N_MIXERS, i // N_MIXERS
        g = norm_mix_g[i]
        y, w_out = None, None
        if kind == 0:
            y = _mixer_rglru(x2, B, S, g, a_w_in[j], a_conv_w[j], a_conv_b[j], a_gate_w[j], a_gate_b[j], a_lambda[j])
            w_out = a_w_out[j].astype(BF16)
        elif kind == 1:
            x2 = _sgu(x2, g, b_w_in[j], b_ln_g[j], b_ln_b[j], b_w_s[j], b_b_s[j], b_w_out[j])
        else:
            y = _mixer_gdn(x2, B, S, g, c_w_in[j], c_conv_w[j], c_a_log[j], c_dt_bias[j], c_norm_g[j])
            w_out = c_w_out[j].astype(BF16)
        x2 = _mlp(x2, norm_mlp_g[i], mlp_w_up[i].astype(BF16), mlp_w_down[i].astype(BF16),
                  norm_final_g if i == depth - 1 else None, y, w_out)
    return x2.reshape(B, S, D_MODEL)
```

```python
import functools

import jax
import jax.numpy as jnp
from jax import lax
from jax.experimental import pallas as pl
from jax.experimental.pallas import tpu as pltpu

F32 = jnp.float32
BF16 = jnp.bfloat16

D_MODEL = 1024
D_FF = 4 * D_MODEL
N_MIXERS = 3
LANES = 128
SUBLANES = 8
VMEM_LIMIT = 56 * 1024 * 1024

HA = 8
HDA = D_MODEL // HA
CONV_W = 4
CONV_LEFT = CONV_W // 2
RG_C = 8.0
LOG2E = 1.4426950408889634
RG_SEG = 64
SEG_PAD = 8
RG_ROWS = 256
SGU_CHUNK = 128
GB = 8
DGB = D_MODEL // GB
HC = 8
DK = 128
DV = 128
DC = HC * DV
GDN_CHUNK = LANES
GDN_BLOCK = GDN_CHUNK
GDN_INV_BASE = 8
GDN_LOCAL_UNROLL = 8


def _params(n_parallel):
    return pltpu.CompilerParams(dimension_semantics=("parallel",) * n_parallel,
                                vmem_limit_bytes=VMEM_LIMIT)


def _rms(xf, g, eps=1e-6):
    return xf * lax.rsqrt(jnp.mean(xf * xf, axis=-1, keepdims=True) + eps) * g


def _sigmoid(x):
    return 0.5 * jnp.tanh(0.5 * x) + 0.5


def _silu(x):
    return x * _sigmoid(x)


def _gelu_tanh(x):
    c = 0.7978845608028654
    return 0.5 * x * (1.0 + jnp.tanh(c * (x + 0.044715 * (x * x * x))))


def _softplus(x):
    return jnp.maximum(x, 0.0) + jnp.log1p(jnp.exp(-jnp.abs(x)))


def _dot(a, b):
    return jnp.dot(a, b, preferred_element_type=F32)


def _dot_nt(a, b):
    return lax.dot_general(a, b, (((1,), (1,)), ((), ())), preferred_element_type=F32)


MLP_TM = 512
MLP_FC = 1024


def _mlp_kernel(x_ref, g_ref, wup_ref, wdn_ref, *rest, final_norm, mixer_proj):
    rest = list(rest)
    o_ref = rest.pop()
    x = x_ref[...]
    if mixer_proj:
        y_ref, wo_ref = rest[:2]
        rest = rest[2:]
        x = x + _dot(y_ref[...], wo_ref[...])
    if final_norm:
        (gf_ref,) = rest
    h = _rms(x, g_ref[...]).astype(BF16)
    acc = x
    for c in range(D_FF // MLP_FC):
        u = _dot(h, wup_ref[:, c * MLP_FC:(c + 1) * MLP_FC])
        a = jnp.square(jnp.maximum(u, 0.0)).astype(BF16)
        acc = acc + _dot(a, wdn_ref[c * MLP_FC:(c + 1) * MLP_FC, :])
    if final_norm:
        acc = _rms(acc, gf_ref[...])
    o_ref[...] = acc


def _const_spec(shape):
    n = len(shape)
    return pl.BlockSpec(shape, lambda *_: (0,) * n, pipeline_mode=pl.Buffered(1))


def _mlp(x2, g, w_up, w_down, g_final=None, mixer_y=None, mixer_w_out=None):
    T = x2.shape[0]
    tok = pl.BlockSpec((MLP_TM, D_MODEL), lambda i: (i, 0))
    in_specs = [tok, _const_spec((1, D_MODEL)), _const_spec((D_MODEL, D_FF)), _const_spec((D_FF, D_MODEL))]
    args = [x2, g.reshape(1, D_MODEL), w_up, w_down]
    if mixer_y is not None:
        in_specs += [tok, _const_spec((D_MODEL, D_MODEL))]
        args += [mixer_y, mixer_w_out]
    if g_final is not None:
        in_specs.append(_const_spec((1, D_MODEL)))
        args.append(g_final.reshape(1, D_MODEL))
    return pl.pallas_call(
        functools.partial(_mlp_kernel, final_norm=g_final is not None, mixer_proj=mixer_y is not None),
        grid=(T // MLP_TM,),
        in_specs=in_specs,
        out_specs=tok,
        out_shape=jax.ShapeDtypeStruct((T, D_MODEL), F32),
        compiler_params=_params(1),
        name="mlp",
    )(*args)


PROJ_TM = 512


def _inproj_kernel(x_ref, g_ref, w_ref, *o_refs, splits):
    h = _rms(x_ref[...], g_ref[...]).astype(BF16)
    for o_ref, (c0, c1) in zip(o_refs, splits):
        o_ref[...] = _dot(h, w_ref[:, c0:c1]).astype(o_ref.dtype)


def _inproj(x2, g, w, splits, dtypes):
    T = x2.shape[0]
    n_out = w.shape[1]
    tok = pl.BlockSpec((PROJ_TM, D_MODEL), lambda i: (i, 0))
    return pl.pallas_call(
        functools.partial(_inproj_kernel, splits=splits),
        grid=(T // PROJ_TM,),
        in_specs=[tok, _const_spec((1, D_MODEL)), _const_spec((D_MODEL, n_out))],
        out_specs=[pl.BlockSpec((PROJ_TM, c1 - c0), lambda i: (i, 0)) for c0, c1 in splits],
        out_shape=[jax.ShapeDtypeStruct((T, c1 - c0), dt) for (c0, c1), dt in zip(splits, dtypes)],
        compiler_params=_params(1),
        name="inproj",
    )(x2, g.reshape(1, D_MODEL), w)


SGU_TM = 512


def _sgu_kernel(x_ref, g_ref, win_ref, lng_ref, lnb_ref, ws_ref, bs_ref, wout_ref, o_ref, y_ref):
    x = x_ref[...]
    h = _rms(x, g_ref[...]).astype(BF16)
    u = _gelu_tanh(_dot(h, win_ref[:, :D_MODEL]))
    v = _gelu_tanh(_dot(h, win_ref[:, D_MODEL:]))
    mu = jnp.mean(v, axis=-1, keepdims=True)
    vc = v - mu
    var = jnp.mean(vc * vc, axis=-1, keepdims=True)
    v = (vc * lax.rsqrt(var + 1e-5) * lng_ref[...] + lnb_ref[...]).astype(BF16)
    n_chunks = SGU_TM // SGU_CHUNK
    for gi in range(GB):
        cols = slice(gi * DGB, (gi + 1) * DGB)
        vg = jnp.concatenate([v[c * SGU_CHUNK:(c + 1) * SGU_CHUNK, cols] for c in range(n_chunks)], axis=1)
        vs = _dot(ws_ref[gi], vg)
        for c in range(n_chunks):
            rows = slice(c * SGU_CHUNK, (c + 1) * SGU_CHUNK)
            vsc = vs[:, c * DGB:(c + 1) * DGB] + bs_ref[gi]
            y_ref[rows, cols] = (u[rows, cols] * vsc).astype(BF16)
    o_ref[...] = x + _dot(y_ref[...], wout_ref[...])


def _sgu(x2, g, w_in, ln_g, ln_b, w_s, b_s, w_out):
    T = x2.shape[0]
    tok = pl.BlockSpec((SGU_TM, D_MODEL), lambda i: (i, 0))
    bs = jnp.broadcast_to(b_s[:, :, None], (GB, SGU_CHUNK, DGB)).astype(F32)
    return pl.pallas_call(
        _sgu_kernel,
        grid=(T // SGU_TM,),
        in_specs=[tok, _const_spec((1, D_MODEL)), _const_spec((D_MODEL, 2 * D_MODEL)),
                  _const_spec((1, D_MODEL)), _const_spec((1, D_MODEL)),
                  _const_spec((GB, SGU_CHUNK, SGU_CHUNK)), _const_spec((GB, SGU_CHUNK, DGB)),
                  _const_spec((D_MODEL, D_MODEL))],
        out_specs=tok,
        out_shape=jax.ShapeDtypeStruct((T, D_MODEL), F32),
        scratch_shapes=[pltpu.VMEM((SGU_TM, D_MODEL), BF16)],
        compiler_params=_params(1),
        name="sgu",
    )(x2, g.reshape(1, D_MODEL), w_in.astype(BF16), ln_g.reshape(1, D_MODEL), ln_b.reshape(1, D_MODEL),
      w_s.astype(BF16), bs, w_out.astype(BF16))


def _rglru_kernel(xr_ref, gate_ref, cw_ref, cb_ref, gw_ref, gb_ref, lam_ref, o_ref,
                  xpad_ref, a_ref, b_ref, h_ref, p_ref, *, seq):
    seg = RG_SEG
    pitch = seg + SEG_PAD
    n_seg = seq // seg
    n_grp = n_seg // SUBLANES
    seg_per_blk = RG_ROWS // seg
    pad = SUBLANES

    zeros = jnp.zeros((pad, HDA), F32)
    xpad_ref[0:pad, :] = zeros
    xpad_ref[pad + seq:pad + seq + pad, :] = zeros
    xpad_ref[pad:pad + seq, :] = xr_ref[0]

    cw = cw_ref[...]
    cb = cb_ref[...]
    gw = gw_ref[0]
    gbias = gb_ref[0]
    hn = [(-0.5 * RG_C * LOG2E) * _softplus(-lam_ref[d:d + 1, :]) for d in range(2)]

    for rb in range(seq // RG_ROWS):
        r0 = rb * RG_ROWS
        xc = cb
        for k in range(CONV_W):
            xc = xc + xpad_ref[pl.ds(pad + r0 + k - CONV_LEFT, RG_ROWS), :] * cw[k:k + 1, :]
        gates = jnp.tanh(_dot(xc.astype(BF16), gw) + gbias)
        xh = 0.5 * xc
        for d in range(2):
            t_r = gates[:, (2 * d) * HDA:(2 * d + 1) * HDA]
            t_i = gates[:, (2 * d + 1) * HDA:(2 * d + 2) * HDA]
            a = jnp.exp2(t_r * hn[d] + hn[d])
            y = 1.0 - a * a
            bx = (y * lax.rsqrt(jnp.maximum(y, 1e-30))) * ((t_i + 1.0) * xh)
            for j in range(seg_per_blk):
                q = rb * seg_per_blk + j
                a_ref[d, q * pitch:q * pitch + seg, :] = a[j * seg:(j + 1) * seg]
                b_ref[d, q * pitch:q * pitch + seg, :] = bx[j * seg:(j + 1) * seg]

    def scan_step(i, carry):
        out = []
        j = seg - 1 - i
        for g in range(n_grp):
            hf, pf, hb, pb = carry[4 * g:4 * g + 4]
            base = g * SUBLANES * pitch
            fwd = pl.ds(base + i, SUBLANES, stride=pitch)
            rev = pl.ds(base + j, SUBLANES, stride=pitch)
            af = a_ref[0, fwd, :]
            ab = a_ref[1, rev, :]
            hf = af * hf + b_ref[0, fwd, :]
            pf = af * pf
            hb = ab * hb + b_ref[1, rev, :]
            pb = ab * pb
            h_ref[0, fwd, :] = hf
            p_ref[0, fwd, :] = pf
            h_ref[1, rev, :] = hb
            p_ref[1, rev, :] = pb
            out += [hf, pf, hb, pb]
        return tuple(out)

    z = jnp.zeros((SUBLANES, HDA), F32)
    o = jnp.ones((SUBLANES, HDA), F32)
    ends = lax.fori_loop(0, seg, scan_step, (z, o, z, o) * n_grp, unroll=2)

    row = lax.broadcasted_iota(jnp.int32, (SUBLANES, HDA), 0)
    cf = [None] * n_grp
    cr = [None] * n_grp
    c_in = jnp.zeros((1, HDA), F32)
    for g in range(n_grp):
        hf, pf = ends[4 * g], ends[4 * g + 1]
        c = jnp.broadcast_to(c_in, (SUBLANES, HDA))
        for _ in range(SUBLANES - 1):
            c = jnp.where(row == 0, c_in, pltpu.roll(hf + pf * c, 1, axis=0))
        cf[g] = c
        c_in = (hf + pf * c)[SUBLANES - 1:SUBLANES, :]
    c_in = jnp.zeros((1, HDA), F32)
    for g in reversed(range(n_grp)):
        hb, pb = ends[4 * g + 2], ends[4 * g + 3]
        c = jnp.broadcast_to(c_in, (SUBLANES, HDA))
        for _ in range(SUBLANES - 1):
            c = jnp.where(row == SUBLANES - 1, c_in, pltpu.roll(hb + pb * c, SUBLANES - 1, axis=0))
        cr[g] = c
        c_in = (hb + pb * c)[0:1, :]

    for q in range(n_seg):
        g, s = divmod(q, SUBLANES)
        rows = slice(q * pitch, q * pitch + seg)
        yf = h_ref[0, rows, :] + p_ref[0, rows, :] * cf[g][s:s + 1, :]
        yb = h_ref[1, rows, :] + p_ref[1, rows, :] * cr[g][s:s + 1, :]
        gate = gate_ref[0, q * seg:(q + 1) * seg, :].astype(F32)
        o_ref[0, q * seg:(q + 1) * seg, :] = ((yf + yb) * _gelu_tanh(gate)).astype(o_ref.dtype)


def _rglru_core(gate3, xr3, conv_w, conv_b, gate_w, gate_b, lam):
    B, S, _ = xr3.shape
    rows = (S // RG_SEG) * (RG_SEG + SEG_PAD)
    blk = pl.BlockSpec((1, S, HDA), lambda b, h: (b, 0, h))
    gw = (0.5 * gate_w).transpose(2, 3, 0, 1, 4).reshape(HA, HDA, 4 * HDA).astype(BF16)
    gb = (0.5 * gate_b).transpose(2, 0, 1, 3).reshape(HA, 1, 4 * HDA)
    return pl.pallas_call(
        functools.partial(_rglru_kernel, seq=S),
        grid=(B, HA),
        in_specs=[blk, blk,
                  pl.BlockSpec((CONV_W, HDA), lambda b, h: (0, h)),
                  pl.BlockSpec((1, HDA), lambda b, h: (0, h)),
                  pl.BlockSpec((1, HDA, 4 * HDA), lambda b, h: (h, 0, 0)),
                  pl.BlockSpec((1, 1, 4 * HDA), lambda b, h: (h, 0, 0)),
                  pl.BlockSpec((2, HDA), lambda b, h: (0, h))],
        out_specs=blk,
        out_shape=jax.ShapeDtypeStruct((B, S, D_MODEL), BF16),
        scratch_shapes=[pltpu.VMEM((S + 2 * SUBLANES, HDA), F32),
                        pltpu.VMEM((2, rows, HDA), F32), pltpu.VMEM((2, rows, HDA), F32),
                        pltpu.VMEM((2, rows, HDA), F32), pltpu.VMEM((2, rows, HDA), F32)],
        compiler_params=_params(2),
        name="rglru",
    )(xr3, gate3, conv_w, conv_b.reshape(1, D_MODEL), gw, gb, lam)


def _mixer_rglru(x2, B, S, g, w_in, conv_w, conv_b, gate_w, gate_b, lam):
    gate, xr = _inproj(x2, g, w_in.astype(BF16), ((0, D_MODEL), (D_MODEL, 2 * D_MODEL)), (BF16, F32))
    y = _rglru_core(gate.reshape(B, S, D_MODEL), xr.reshape(B, S, D_MODEL),
                    conv_w, conv_b, gate_w, gate_b, lam)
    return y.reshape(B * S, D_MODEL)


ROW_BETA, ROW_GC, ROW_EG, ROW_EKD, ROW_ETOT = 2, 8, 16, 24, 32


def _inproj_t_kernel(x_ref, g_ref, w_ref, wt_ref, *o_refs, splits):
    h = _rms(x_ref[...], g_ref[...]).astype(BF16)
    for o_ref, (c0, c1) in zip(o_refs[:-1], splits):
        o_ref[...] = _dot(h, w_ref[:, c0:c1]).astype(o_ref.dtype)
    o_refs[-1][...] = _dot_nt(wt_ref[...], h)


def _inproj_t(x2, g, w, wt, splits, dtypes):
    T = x2.shape[0]
    n_out = w.shape[1]
    n_t = wt.shape[0]
    tok = pl.BlockSpec((PROJ_TM, D_MODEL), lambda i: (i, 0))
    return pl.pallas_call(
        functools.partial(_inproj_t_kernel, splits=splits),
        grid=(T // PROJ_TM,),
        in_specs=[tok, _const_spec((1, D_MODEL)), _const_spec((D_MODEL, n_out)), _const_spec((n_t, D_MODEL))],
        out_specs=[pl.BlockSpec((PROJ_TM, c1 - c0), lambda i: (i, 0)) for c0, c1 in splits]
        + [pl.BlockSpec((n_t, PROJ_TM), lambda i: (0, i))],
        out_shape=[jax.ShapeDtypeStruct((T, c1 - c0), dt) for (c0, c1), dt in zip(splits, dtypes)]
        + [jax.ShapeDtypeStruct((n_t, T), F32)],
        compiler_params=_params(1),
        name="inproj_t",
    )(x2, g.reshape(1, D_MODEL), w, wt)


def _gdn_kernel(q_ref, k_ref, v_ref, gate_ref, ab_ref, alog_ref, dtb_ref, cwq_ref, cwk_ref, cwv_ref, ng_ref,
                o_ref,
                xpad_ref, qn_ref, kn_ref, vv_ref, cf_ref, gcrow_ref, g_ref, bm_ref, qt_ref, qu_ref,
                *, seq):
    pad = SUBLANES
    n_blk = seq // GDN_BLOCK
    C = GDN_CHUNK
    RB = 256

    zeros = jnp.zeros((pad, DK), F32)
    xpad_ref[0:pad, :] = zeros
    xpad_ref[pad + seq:pad + seq + pad, :] = zeros

    def conv_silu(src_ref, cw_ref, dst_ref, normalize, scale):
        xpad_ref[pad:pad + seq, :] = src_ref[0]
        cw = 0.5 * cw_ref[...]
        for rb in range(seq // RB):
            r0 = rb * RB
            acc = xpad_ref[pl.ds(pad + r0 - CONV_LEFT, RB), :] * cw[0:1, :]
            for k in range(1, CONV_W):
                acc = acc + xpad_ref[pl.ds(pad + r0 + k - CONV_LEFT, RB), :] * cw[k:k + 1, :]
            y = acc * (jnp.tanh(acc) + 1.0)
            if normalize:
                y = y * (lax.rsqrt(jnp.sum(y * y, axis=-1, keepdims=True) + 1e-6) * scale)
            dst_ref[r0:r0 + RB, :] = y

    conv_silu(q_ref, cwq_ref, qn_ref, True, DK ** -0.5)
    conv_silu(k_ref, cwk_ref, kn_ref, True, 1.0)
    conv_silu(v_ref, cwv_ref, vv_ref, False, 1.0)

    ab = ab_ref[0]
    g8 = -jnp.exp(alog_ref[0][:, 0:1]) * _softplus(ab + dtb_ref[0][:, 0:1])
    beta8 = _sigmoid(ab)
    lane = lax.broadcasted_iota(jnp.int32, (SUBLANES, seq), 1) % C
    pre = g8
    suf = g8
    sh = 1
    while sh < C:
        pre = pre + jnp.where(lane >= sh, pltpu.roll(pre, sh, axis=1), 0.0)
        suf = suf + jnp.where(lane < C - sh, pltpu.roll(suf, seq - sh, axis=1), 0.0)
        sh *= 2
    tot = pre + suf - g8
    row8 = lax.broadcasted_iota(jnp.int32, (SUBLANES, seq), 0)
    gcs = jnp.where(row8 == 0, pre, suf)
    eg = jnp.exp(gcs)
    ekd = jnp.exp(tot - gcs)
    etot = jnp.exp(tot)
    fill = jnp.zeros((GDN_BLOCK - 5 * SUBLANES, GDN_BLOCK), F32)
    for b in range(n_blk):
        ls = slice(b * GDN_BLOCK, (b + 1) * GDN_BLOCK)
        table = jnp.concatenate([beta8[:, ls], gcs[:, ls], eg[:, ls], ekd[:, ls], etot[:, ls], fill], axis=0)
        cf_ref[b * GDN_BLOCK:(b + 1) * GDN_BLOCK, :] = table.T
        gcrow_ref[b] = gcs[:, ls]

    ri = lax.broadcasted_iota(jnp.int32, (C, C), 0)
    ci = lax.broadcasted_iota(jnp.int32, (C, C), 1)
    masks = (ri >= ci, ri <= ci)
    stricts = (ri > ci, ri < ci)
    eye = (ri == ci).astype(F32)
    base_blk = (ri // GDN_INV_BASE) == (ci // GDN_INV_BASE)
    sub_sizes = []
    s = GDN_INV_BASE
    while s < C:
        sub_sizes.append(s)
        s *= 2

    def take_rows(x, s, odd):
        off = s if odd else 0
        return jnp.concatenate([x[k + off:k + off + s] for k in range(0, x.shape[0], 2 * s)], axis=0)

    def put_rows(xc, s, odd):
        z = jnp.zeros((s, xc.shape[1]), xc.dtype)
        pieces = []
        for k in range(0, xc.shape[0], s):
            pieces += [z, xc[k:k + s]] if odd else [xc[k:k + s], z]
        return jnp.concatenate(pieces, axis=0)

    rh = lax.broadcasted_iota(jnp.int32, (C // 2, C), 0)
    ch_ = lax.broadcasted_iota(jnp.int32, (C // 2, C), 1)

    def couple_mask(s, odd):
        row = (rh // s) * (2 * s) + (s if odd else 0) + rh % s
        return ((row // (2 * s)) == (ch_ // (2 * s))) & ((row // s) != (ch_ // s))

    couple = tuple(tuple(couple_mask(s, odd) for s in sub_sizes) for odd in (True, False))

    def local_step(i, carry):
        chains = []
        for bb in range(GDN_LOCAL_UNROLL):
            b = i * GDN_LOCAL_UNROLL + bb
            r0 = pl.multiple_of(b * C, C)
            kn = kn_ref[pl.ds(r0, C), :]
            qn = qn_ref[pl.ds(r0, C), :]
            kq = _dot_nt(jnp.concatenate([kn, qn], axis=0).astype(BF16), kn.astype(BF16))
            for d in range(2):
                chains.append(dict(b=b, r0=r0, d=d, kk=kq[:C], qk=kq[C:]))
        for ch in chains:
            d = ch["d"]
            cf = cf_ref[pl.ds(ch["r0"], C), :]
            gcr = gcrow_ref[ch["b"]]
            beta_c = cf[:, ROW_BETA + d:ROW_BETA + d + 1]
            gc_c = cf[:, ROW_GC + d:ROW_GC + d + 1]
            decay = jnp.where(masks[d], jnp.exp(jnp.where(masks[d], gc_c - gcr[d:d + 1, :], 0.0)), 0.0)
            n_mat = jnp.where(stricts[d], -(beta_c * ch["kk"] * decay), 0.0)
            ch["qkd"] = jnp.where(masks[d], ch["qk"] * decay, 0.0).astype(BF16)
            ch["n"] = n_mat
            n0 = jnp.where(base_blk, n_mat, 0.0)
            ch["t"] = eye + n0
            ch["p16"] = n0.astype(BF16)
        for ch in chains:
            ch["p16"] = _dot(ch["p16"], ch["p16"]).astype(BF16)
        for ch in chains:
            pt = _dot(ch["p16"], jnp.concatenate([ch["p16"], ch["t"].astype(BF16)], axis=1))
            ch["p16"] = pt[:, :C].astype(BF16)
            ch["t"] = ch["t"] + pt[:, C:]
        for ch in chains:
            ch["t"] = ch["t"] + _dot(ch["p16"], ch["t"].astype(BF16))
        for lvl, s in enumerate(sub_sizes):
            for ch in chains:
                odd = ch["d"] == 0
                e16 = jnp.where(couple[ch["d"]][lvl], take_rows(ch["n"], s, odd), 0.0).astype(BF16)
                t16 = ch["t"].astype(BF16)
                ch["et"] = put_rows(_dot(e16, t16).astype(BF16), s, odd)
                ch["th"] = take_rows(t16, s, odd)
            for ch in chains:
                ch["t"] = ch["t"] + put_rows(_dot(ch["th"], ch["et"]), s, ch["d"] == 0)
        for ch in chains:
            d = ch["d"]
            rows = pl.ds(ch["r0"], C)
            cf = cf_ref[rows, :]
            kn = kn_ref[rows, :]
            beta_c = cf[:, ROW_BETA + d:ROW_BETA + d + 1]
            eg_c = cf[:, ROW_EG + d:ROW_EG + d + 1]
            rhs = jnp.concatenate([kn * (beta_c * eg_c), vv_ref[rows, :] * beta_c], axis=1).astype(BF16)
            ch["wu"] = _dot(ch["t"].astype(BF16), rhs).astype(BF16)
        for ch in chains:
            d, b = ch["d"], ch["b"]
            rows = pl.ds(ch["r0"], C)
            cf = cf_ref[rows, :]
            eg_c = cf[:, ROW_EG + d:ROW_EG + d + 1]
            ekd_c = cf[:, ROW_EKD + d:ROW_EKD + d + 1]
            kdt = (kn_ref[rows, :] * ekd_c).T.astype(BF16)
            gb = _dot(kdt, ch["wu"])
            g_ref[d, b] = (-gb[:, :DK]).astype(BF16)
            bm_ref[d, b] = gb[:, DK:]
            qwu = _dot(ch["qkd"], ch["wu"])
            qt_ref[d, rows, :] = (qn_ref[rows, :] * eg_c - qwu[:, :DK]).astype(BF16)
            qu_ref[d, rows, :] = qwu[:, DK:]
        return carry

    lax.fori_loop(0, n_blk // GDN_LOCAL_UNROLL, local_step, 0)

    n_chunks = n_blk

    def state_step(n, carry):
        states = list(carry)
        chunks = (n, n_chunks - 1 - n)
        s16 = [s.astype(BF16) for s in states]
        for d in range(2):
            r0 = pl.multiple_of(chunks[d] * C, C)
            et = cf_ref[pl.ds(r0, 1), ROW_ETOT + d:ROW_ETOT + d + 1]
            states[d] = states[d] * et + (_dot(g_ref[d, chunks[d]], s16[d]) + bm_ref[d, chunks[d]])
        for d in range(2):
            rows = pl.ds(pl.multiple_of(chunks[d] * C, C), C)
            qu_ref[d, rows, :] = qu_ref[d, rows, :] + _dot(qt_ref[d, rows, :], s16[d])
        return tuple(states)

    s0 = jnp.zeros((DK, DV), F32)
    lax.fori_loop(0, n_chunks, state_step, (s0, s0), unroll=2)

    ng = ng_ref[...]
    for rb in range(seq // RB):
        rows = slice(rb * RB, (rb + 1) * RB)
        o = qu_ref[0, rows, :] + qu_ref[1, rows, :]
        gate = gate_ref[0, rows, :].astype(F32)
        o_ref[0, rows, :] = (_rms(o, ng) * _silu(gate)).astype(o_ref.dtype)


def _gdn_core(qkv3, gate3, abt3, alog8, dtb8, conv_w, norm_g):
    B, S, _ = qkv3.shape
    n_blk = S // GDN_BLOCK
    n_chunks = S // GDN_CHUNK

    def col(off):
        return pl.BlockSpec((1, S, DK), lambda b, h: (b, 0, off + h))

    def cw(off):
        return pl.BlockSpec((CONV_W, DK), lambda b, h: (0, off + h))

    par = pl.BlockSpec((1, SUBLANES, LANES), lambda b, h: (h, 0, 0))
    return pl.pallas_call(
        functools.partial(_gdn_kernel, seq=S),
        grid=(B, HC),
        in_specs=[col(0), col(HC), col(2 * HC),
                  pl.BlockSpec((1, S, DV), lambda b, h: (b, 0, h)),
                  pl.BlockSpec((1, SUBLANES, S), lambda b, h: (h, 0, b)),
                  par, par, cw(0), cw(HC), cw(2 * HC),
                  pl.BlockSpec((1, DV), lambda b, h: (0, 0))],
        out_specs=pl.BlockSpec((1, S, DV), lambda b, h: (b, 0, h)),
        out_shape=jax.ShapeDtypeStruct((B, S, DC), BF16),
        scratch_shapes=[pltpu.VMEM((S + 2 * SUBLANES, DK), F32),
                        pltpu.VMEM((S, DK), F32), pltpu.VMEM((S, DK), F32), pltpu.VMEM((S, DV), F32),
                        pltpu.VMEM((S, LANES), F32), pltpu.VMEM((n_blk, SUBLANES, GDN_BLOCK), F32),
                        pltpu.VMEM((2, n_chunks, DK, DK), BF16), pltpu.VMEM((2, n_chunks, DK, DV), F32),
                        pltpu.VMEM((2, S, DK), BF16), pltpu.VMEM((2, S, DV), F32)],
        compiler_params=_params(2),
        name="gdn",
    )(qkv3, qkv3, qkv3, gate3, abt3, alog8, dtb8, conv_w, conv_w, conv_w, norm_g.reshape(1, DV))


def _mixer_gdn(x2, B, S, g, w_in, conv_w, a_log, dt_bias, norm_g):
    w_ab = w_in[:, 4 * DC:].reshape(D_MODEL, 2, 2, HC).transpose(3, 1, 2, 0).reshape(HC, 4, D_MODEL)
    w_ab = jnp.concatenate([w_ab, jnp.zeros_like(w_ab)], axis=1).reshape(HC * SUBLANES, D_MODEL)
    qkv, gate, abt = _inproj_t(x2, g, w_in[:, :4 * DC].astype(BF16), w_ab.astype(BF16),
                               ((0, 3 * DC), (3 * DC, 4 * DC)), (F32, BF16))

    def per_head(p):
        p8 = jnp.concatenate([p.T.astype(F32), jnp.zeros((HC, SUBLANES - 2), F32)], axis=1)
        return jnp.broadcast_to(p8[:, :, None], (HC, SUBLANES, LANES))

    y = _gdn_core(qkv.reshape(B, S, 3 * DC), gate.reshape(B, S, DC), abt.reshape(HC, SUBLANES, B * S),
                  per_head(a_log), per_head(dt_bias), conv_w, norm_g)
    return y.reshape(B * S, DC)


def kernel(x, norm_mix_g, norm_mlp_g, mlp_w_up, mlp_w_down, norm_final_g, a_w_in, a_conv_w, a_conv_b, a_gate_w, a_gate_b, a_lambda, a_w_out, b_w_in, b_ln_g, b_ln_b, b_w_s, b_b_s, b_w_out, c_w_in, c_conv_w, c_a_log, c_dt_bias, c_norm_g, c_w_out):
    B, S, _ = x.shape
    depth = norm_mix_g.shape[0]
    x2 = x.reshape(B * S, D_MODEL)
    for i in range(depth):
        kind, j = i % N_MIXERS, i // N_MIXERS
        g = norm_mix_g[i]
        y, w_out = None, None
        if kind == 0:
            y = _mixer_rglru(x2, B, S, g, a_w_in[j], a_conv_w[j], a_conv_b[j], a_gate_w[j], a_gate_b[j], a_lambda[j])
            w_out = a_w_out[j].astype(BF16)
        elif kind == 1:
            x2 = _sgu(x2, g, b_w_in[j], b_ln_g[j], b_ln_b[j], b_w_s[j], b_b_s[j], b_w_out[j])
        else:
            y = _mixer_gdn(x2, B, S, g, c_w_in[j], c_conv_w[j], c_a_log[j], c_dt_bias[j], c_norm_g[j])
            w_out = c_w_out[j].astype(BF16)
        x2 = _mlp(x2, norm_mlp_g[i], mlp_w_up[i].astype(BF16), mlp_w_down[i].astype(BF16),
                  norm_final_g if i == depth - 1 else None, y, w_out)
    return x2.reshape(B, S, D_MODEL)
```

```python
import functools

import jax
import jax.numpy as jnp
from jax import lax
from jax.experimental import pallas as pl
from jax.experimental.pallas import tpu as pltpu

F32 = jnp.float32
BF16 = jnp.bfloat16

D_MODEL = 1024
D_FF = 4 * D_MODEL
N_MIXERS = 3
LANES = 128
SUBLANES = 8
VMEM_LIMIT = 56 * 1024 * 1024

HA = 8
HDA = D_MODEL // HA
CONV_W = 4
CONV_LEFT = CONV_W // 2
RG_C = 8.0
LOG2E = 1.4426950408889634
RG_SEG = 64
SEG_PAD = 8
RG_ROWS = 256
SGU_CHUNK = 128
GB = 8
DGB = D_MODEL // GB
HC = 8
DK = 128
DV = 128
DC = HC * DV
GDN_CHUNK = LANES
GDN_BLOCK = GDN_CHUNK
GDN_INV_BASE = 8
GDN_LOCAL_UNROLL = 8
GDN_STATE_EVERY = 1


def _params(n_parallel):
    return pltpu.CompilerParams(dimension_semantics=("parallel",) * n_parallel,
                                vmem_limit_bytes=VMEM_LIMIT)


def _rms(xf, g, eps=1e-6):
    return xf * lax.rsqrt(jnp.mean(xf * xf, axis=-1, keepdims=True) + eps) * g


def _sigmoid(x):
    return 0.5 * jnp.tanh(0.5 * x) + 0.5


def _silu(x):
    return x * _sigmoid(x)


def _gelu_tanh(x):
    c = 0.7978845608028654
    return 0.5 * x * (1.0 + jnp.tanh(c * (x + 0.044715 * (x * x * x))))


def _softplus(x):
    return jnp.maximum(x, 0.0) + jnp.log1p(jnp.exp(-jnp.abs(x)))


def _dot(a, b):
    return jnp.dot(a, b, preferred_element_type=F32)


def _aligned(offset, multiple):
    return offset if isinstance(offset, int) else pl.multiple_of(offset, multiple)


def _dot_nt(a, b):
    return lax.dot_general(a, b, (((1,), (1,)), ((), ())), preferred_element_type=F32)


MLP_TM = 512
MLP_FC = 1024


def _mlp_kernel(x_ref, g_ref, wup_ref, wdn_ref, *rest, final_norm, mixer_proj):
    rest = list(rest)
    o_ref = rest.pop()
    x = x_ref[...]
    if mixer_proj:
        y_ref, wo_ref = rest[:2]
        rest = rest[2:]
        x = x + _dot(y_ref[...], wo_ref[...])
    if final_norm:
        (gf_ref,) = rest
    h = _rms(x, g_ref[...]).astype(BF16)
    acc = x
    for c in range(D_FF // MLP_FC):
        u = _dot(h, wup_ref[:, c * MLP_FC:(c + 1) * MLP_FC])
        a = jnp.square(jnp.maximum(u, 0.0)).astype(BF16)
        acc = acc + _dot(a, wdn_ref[c * MLP_FC:(c + 1) * MLP_FC, :])
    if final_norm:
        acc = _rms(acc, gf_ref[...])
    o_ref[...] = acc


def _const_spec(shape):
    n = len(shape)
    return pl.BlockSpec(shape, lambda *_: (0,) * n, pipeline_mode=pl.Buffered(1))


def _mlp(x2, g, w_up, w_down, g_final=None, mixer_y=None, mixer_w_out=None):
    T = x2.shape[0]
    tok = pl.BlockSpec((MLP_TM, D_MODEL), lambda i: (i, 0))
    in_specs = [tok, _const_spec((1, D_MODEL)), _const_spec((D_MODEL, D_FF)), _const_spec((D_FF, D_MODEL))]
    args = [x2, g.reshape(1, D_MODEL), w_up, w_down]
    if mixer_y is not None:
        in_specs += [tok, _const_spec((D_MODEL, D_MODEL))]
        args += [mixer_y, mixer_w_out]
    if g_final is not None:
        in_specs.append(_const_spec((1, D_MODEL)))
        args.append(g_final.reshape(1, D_MODEL))
    return pl.pallas_call(
        functools.partial(_mlp_kernel, final_norm=g_final is not None, mixer_proj=mixer_y is not None),
        grid=(T // MLP_TM,),
        in_specs=in_specs,
        out_specs=tok,
        out_shape=jax.ShapeDtypeStruct((T, D_MODEL), F32),
        compiler_params=_params(1),
        name="mlp",
    )(*args)


PROJ_TM = 512


def _inproj_kernel(x_ref, g_ref, w_ref, *o_refs, splits):
    h = _rms(x_ref[...], g_ref[...]).astype(BF16)
    for o_ref, (c0, c1) in zip(o_refs, splits):
        o_ref[...] = _dot(h, w_ref[:, c0:c1]).astype(o_ref.dtype)


def _inproj(x2, g, w, splits, dtypes):
    T = x2.shape[0]
    n_out = w.shape[1]
    tok = pl.BlockSpec((PROJ_TM, D_MODEL), lambda i: (i, 0))
    return pl.pallas_call(
        functools.partial(_inproj_kernel, splits=splits),
        grid=(T // PROJ_TM,),
        in_specs=[tok, _const_spec((1, D_MODEL)), _const_spec((D_MODEL, n_out))],
        out_specs=[pl.BlockSpec((PROJ_TM, c1 - c0), lambda i: (i, 0)) for c0, c1 in splits],
        out_shape=[jax.ShapeDtypeStruct((T, c1 - c0), dt) for (c0, c1), dt in zip(splits, dtypes)],
        compiler_params=_params(1),
        name="inproj",
    )(x2, g.reshape(1, D_MODEL), w)


SGU_TM = 512


def _sgu_kernel(x_ref, g_ref, win_ref, lng_ref, lnb_ref, ws_ref, bs_ref, wout_ref, o_ref, y_ref):
    x = x_ref[...]
    h = _rms(x, g_ref[...]).astype(BF16)
    u = _gelu_tanh(_dot(h, win_ref[:, :D_MODEL]))
    v = _gelu_tanh(_dot(h, win_ref[:, D_MODEL:]))
    mu = jnp.mean(v, axis=-1, keepdims=True)
    vc = v - mu
    var = jnp.mean(vc * vc, axis=-1, keepdims=True)
    v = (vc * lax.rsqrt(var + 1e-5) * lng_ref[...] + lnb_ref[...]).astype(BF16)
    n_chunks = SGU_TM // SGU_CHUNK
    for gi in range(GB):
        cols = slice(gi * DGB, (gi + 1) * DGB)
        vg = jnp.concatenate([v[c * SGU_CHUNK:(c + 1) * SGU_CHUNK, cols] for c in range(n_chunks)], axis=1)
        vs = _dot(ws_ref[gi], vg)
        for c in range(n_chunks):
            rows = slice(c * SGU_CHUNK, (c + 1) * SGU_CHUNK)
            vsc = vs[:, c * DGB:(c + 1) * DGB] + bs_ref[gi]
            y_ref[rows, cols] = (u[rows, cols] * vsc).astype(BF16)
    o_ref[...] = x + _dot(y_ref[...], wout_ref[...])


def _sgu(x2, g, w_in, ln_g, ln_b, w_s, b_s, w_out):
    T = x2.shape[0]
    tok = pl.BlockSpec((SGU_TM, D_MODEL), lambda i: (i, 0))
    bs = jnp.broadcast_to(b_s[:, :, None], (GB, SGU_CHUNK, DGB)).astype(F32)
    return pl.pallas_call(
        _sgu_kernel,
        grid=(T // SGU_TM,),
        in_specs=[tok, _const_spec((1, D_MODEL)), _const_spec((D_MODEL, 2 * D_MODEL)),
                  _const_spec((1, D_MODEL)), _const_spec((1, D_MODEL)),
                  _const_spec((GB, SGU_CHUNK, SGU_CHUNK)), _const_spec((GB, SGU_CHUNK, DGB)),
                  _const_spec((D_MODEL, D_MODEL))],
        out_specs=tok,
        out_shape=jax.ShapeDtypeStruct((T, D_MODEL), F32),
        scratch_shapes=[pltpu.VMEM((SGU_TM, D_MODEL), BF16)],
        compiler_params=_params(1),
        name="sgu",
    )(x2, g.reshape(1, D_MODEL), w_in.astype(BF16), ln_g.reshape(1, D_MODEL), ln_b.reshape(1, D_MODEL),
      w_s.astype(BF16), bs, w_out.astype(BF16))


def _rglru_kernel(xr_ref, gate_ref, cw_ref, cb_ref, gw_ref, gb_ref, lam_ref, o_ref,
                  xs_ref, xq_ref, a_ref, b_ref, h_ref, p_ref, ys_ref, *, seq):
    seg = RG_SEG
    pitch = seg + SEG_PAD
    n_seg = seq // seg
    n_grp = n_seg // SUBLANES
    halo_lo, halo_hi = CONV_LEFT, CONV_W - 1 - CONV_LEFT
    grp_rows = (seg + halo_lo + halo_hi) * SUBLANES
    row = lax.broadcasted_iota(jnp.int32, (SUBLANES, HDA), 0)

    for q in range(n_seg):
        xs_ref[q * pitch:q * pitch + seg, :] = xr_ref[0, q * seg:(q + 1) * seg, :]

    def to_scan_order(i, carry):
        for g in range(n_grp):
            v = xs_ref[pl.ds(g * SUBLANES * pitch + i, SUBLANES, stride=pitch), :]
            xq_ref[pl.ds(_aligned(g * grp_rows + (halo_lo + i) * SUBLANES, SUBLANES), SUBLANES), :] = v
        return carry

    lax.fori_loop(0, seg, to_scan_order, 0, unroll=8)

    def xq_step(g, i):
        r0 = g * grp_rows + (halo_lo + i) * SUBLANES
        return xq_ref[r0:r0 + SUBLANES, :]

    zrow = jnp.zeros((1, HDA), F32)
    for g in range(n_grp):
        for i in range(seg - halo_lo, seg):
            edge = xq_step(g - 1, i)[SUBLANES - 1:SUBLANES, :] if g > 0 else zrow
            v = jnp.where(row == 0, edge, pltpu.roll(xq_step(g, i), 1, axis=0))
            r0 = g * grp_rows + (halo_lo + i - seg) * SUBLANES
            xq_ref[r0:r0 + SUBLANES, :] = v
        for i in range(halo_hi):
            edge = xq_step(g + 1, i)[0:1, :] if g < n_grp - 1 else zrow
            v = jnp.where(row == SUBLANES - 1, edge, pltpu.roll(xq_step(g, i), SUBLANES - 1, axis=0))
            r0 = g * grp_rows + (halo_lo + seg + i) * SUBLANES
            xq_ref[r0:r0 + SUBLANES, :] = v

    cw = cw_ref[...]
    cb = cb_ref[...]
    gw = gw_ref[0]
    gbias = gb_ref[0]
    hn = [(-0.5 * RG_C * LOG2E) * _softplus(-lam_ref[d:d + 1, :]) for d in range(2)]

    steps_per_blk = RG_ROWS // SUBLANES
    for g in range(n_grp):
        for ib in range(seg // steps_per_blk):
            src = g * grp_rows + ib * RG_ROWS
            dst = (g * seg + ib * steps_per_blk) * SUBLANES
            xc = cb
            for k in range(CONV_W):
                xc = xc + xq_ref[src + k * SUBLANES:src + k * SUBLANES + RG_ROWS, :] * cw[k:k + 1, :]
            gates = jnp.tanh(_dot(xc.astype(BF16), gw) + gbias)
            xh = 0.5 * xc
            for d in range(2):
                t_r = gates[:, (2 * d) * HDA:(2 * d + 1) * HDA]
                t_i = gates[:, (2 * d + 1) * HDA:(2 * d + 2) * HDA]
                a = jnp.exp2(t_r * hn[d] + hn[d])
                y = 1.0 - a * a
                a_ref[d, dst:dst + RG_ROWS, :] = a
                b_ref[d, dst:dst + RG_ROWS, :] = (y * lax.rsqrt(jnp.maximum(y, 1e-30))) * ((t_i + 1.0) * xh)

    def scan_step(i, carry):
        out = []
        row_f = _aligned(i * SUBLANES, SUBLANES)
        row_r = _aligned((seg - 1 - i) * SUBLANES, SUBLANES)
        for g in range(n_grp):
            hf, pf, hb, pb = carry[4 * g:4 * g + 4]
            fwd = pl.ds(row_f + g * seg * SUBLANES, SUBLANES)
            rev = pl.ds(row_r + g * seg * SUBLANES, SUBLANES)
            af = a_ref[0, fwd, :]
            ab = a_ref[1, rev, :]
            hf = af * hf + b_ref[0, fwd, :]
            pf = af * pf
            hb = ab * hb + b_ref[1, rev, :]
            pb = ab * pb
            h_ref[0, fwd, :] = hf
            p_ref[0, fwd, :] = pf
            h_ref[1, rev, :] = hb
            p_ref[1, rev, :] = pb
            out += [hf, pf, hb, pb]
        return tuple(out)

    z = jnp.zeros((SUBLANES, HDA), F32)
    o = jnp.ones((SUBLANES, HDA), F32)
    ends = lax.fori_loop(0, seg, scan_step, (z, o, z, o) * n_grp, unroll=2)

    cf = [None] * n_grp
    cr = [None] * n_grp
    c_in = jnp.zeros((1, HDA), F32)
    for g in range(n_grp):
        hf, pf = ends[4 * g], ends[4 * g + 1]
        c = jnp.broadcast_to(c_in, (SUBLANES, HDA))
        for _ in range(SUBLANES - 1):
            c = jnp.where(row == 0, c_in, pltpu.roll(hf + pf * c, 1, axis=0))
        cf[g] = c
        c_in = (hf + pf * c)[SUBLANES - 1:SUBLANES, :]
    c_in = jnp.zeros((1, HDA), F32)
    for g in reversed(range(n_grp)):
        hb, pb = ends[4 * g + 2], ends[4 * g + 3]
        c = jnp.broadcast_to(c_in, (SUBLANES, HDA))
        for _ in range(SUBLANES - 1):
            c = jnp.where(row == SUBLANES - 1, c_in, pltpu.roll(hb + pb * c, SUBLANES - 1, axis=0))
        cr[g] = c
        c_in = (hb + pb * c)[0:1, :]

    for g in range(n_grp):
        cfb = jnp.tile(cf[g], (steps_per_blk, 1))
        crb = jnp.tile(cr[g], (steps_per_blk, 1))
        for ib in range(seg // steps_per_blk):
            rows = slice((g * seg + ib * steps_per_blk) * SUBLANES, (g * seg + (ib + 1) * steps_per_blk) * SUBLANES)
            ys_ref[rows, :] = ((h_ref[0, rows, :] + p_ref[0, rows, :] * cfb)
                               + (h_ref[1, rows, :] + p_ref[1, rows, :] * crb))

    for q in range(n_seg):
        g, s = divmod(q, SUBLANES)
        y = jnp.concatenate(
            [ys_ref[pl.ds((g * seg + i0) * SUBLANES + s, SUBLANES, stride=SUBLANES), :]
             for i0 in range(0, seg, SUBLANES)], axis=0)
        gate = gate_ref[0, q * seg:(q + 1) * seg, :].astype(F32)
        o_ref[0, q * seg:(q + 1) * seg, :] = (y * _gelu_tanh(gate)).astype(o_ref.dtype)


def _rglru_core(gate3, xr3, conv_w, conv_b, gate_w, gate_b, lam):
    B, S, _ = xr3.shape
    n_seg = S // RG_SEG
    xs_rows = n_seg * (RG_SEG + SEG_PAD)
    xq_rows = (n_seg // SUBLANES) * (RG_SEG + CONV_W - 1) * SUBLANES
    blk = pl.BlockSpec((1, S, HDA), lambda b, h: (b, 0, h))
    gw = (0.5 * gate_w).transpose(2, 3, 0, 1, 4).reshape(HA, HDA, 4 * HDA).astype(BF16)
    gb = (0.5 * gate_b).transpose(2, 0, 1, 3).reshape(HA, 1, 4 * HDA)
    return pl.pallas_call(
        functools.partial(_rglru_kernel, seq=S),
        grid=(B, HA),
        in_specs=[blk, blk,
                  pl.BlockSpec((CONV_W, HDA), lambda b, h: (0, h)),
                  pl.BlockSpec((1, HDA), lambda b, h: (0, h)),
                  pl.BlockSpec((1, HDA, 4 * HDA), lambda b, h: (h, 0, 0)),
                  pl.BlockSpec((1, 1, 4 * HDA), lambda b, h: (h, 0, 0)),
                  pl.BlockSpec((2, HDA), lambda b, h: (0, h))],
        out_specs=blk,
        out_shape=jax.ShapeDtypeStruct((B, S, D_MODEL), BF16),
        scratch_shapes=[pltpu.VMEM((xs_rows, HDA), F32), pltpu.VMEM((xq_rows, HDA), F32),
                        pltpu.VMEM((2, S, HDA), F32), pltpu.VMEM((2, S, HDA), F32),
                        pltpu.VMEM((2, S, HDA), F32), pltpu.VMEM((2, S, HDA), F32),
                        pltpu.VMEM((S, HDA), F32)],
        compiler_params=_params(2),
        name="rglru",
    )(xr3, gate3, conv_w, conv_b.reshape(1, D_MODEL), gw, gb, lam)


def _mixer_rglru(x2, B, S, g, w_in, conv_w, conv_b, gate_w, gate_b, lam):
    gate, xr = _inproj(x2, g, w_in.astype(BF16), ((0, D_MODEL), (D_MODEL, 2 * D_MODEL)), (BF16, F32))
    y = _rglru_core(gate.reshape(B, S, D_MODEL), xr.reshape(B, S, D_MODEL),
                    conv_w, conv_b, gate_w, gate_b, lam)
    return y.reshape(B * S, D_MODEL)


ROW_BETA, ROW_GC, ROW_EG, ROW_EKD, ROW_ETOT = 2, 8, 16, 24, 32


def _inproj_t_kernel(x_ref, g_ref, w_ref, wt_ref, *o_refs, splits):
    h = _rms(x_ref[...], g_ref[...]).astype(BF16)
    for o_ref, (c0, c1) in zip(o_refs[:-1], splits):
        o_ref[...] = _dot(h, w_ref[:, c0:c1]).astype(o_ref.dtype)
    o_refs[-1][...] = _dot_nt(wt_ref[...], h)


def _inproj_t(x2, g, w, wt, splits, dtypes):
    T = x2.shape[0]
    n_out = w.shape[1]
    n_t = wt.shape[0]
    tok = pl.BlockSpec((PROJ_TM, D_MODEL), lambda i: (i, 0))
    return pl.pallas_call(
        functools.partial(_inproj_t_kernel, splits=splits),
        grid=(T // PROJ_TM,),
        in_specs=[tok, _const_spec((1, D_MODEL)), _const_spec((D_MODEL, n_out)), _const_spec((n_t, D_MODEL))],
        out_specs=[pl.BlockSpec((PROJ_TM, c1 - c0), lambda i: (i, 0)) for c0, c1 in splits]
        + [pl.BlockSpec((n_t, PROJ_TM), lambda i: (0, i))],
        out_shape=[jax.ShapeDtypeStruct((T, c1 - c0), dt) for (c0, c1), dt in zip(splits, dtypes)]
        + [jax.ShapeDtypeStruct((n_t, T), F32)],
        compiler_params=_params(1),
        name="inproj_t",
    )(x2, g.reshape(1, D_MODEL), w, wt)


def _gdn_kernel(q_ref, k_ref, v_ref, gate_ref, ab_ref, alog_ref, dtb_ref, cwq_ref, cwk_ref, cwv_ref, ng_ref,
                o_ref,
                xpad_ref, qn_ref, kn_ref, vv_ref, cf_ref, gcrow_ref, g_ref, bm_ref, qt_ref, qu_ref,
                *, seq):
    pad = SUBLANES
    n_blk = seq // GDN_BLOCK
    C = GDN_CHUNK
    RB = 256

    zeros = jnp.zeros((pad, DK), F32)
    xpad_ref[0:pad, :] = zeros
    xpad_ref[pad + seq:pad + seq + pad, :] = zeros

    def conv_silu(src_ref, cw_ref, dst_ref, normalize, scale):
        xpad_ref[pad:pad + seq, :] = src_ref[0]
        cw = 0.5 * cw_ref[...]
        for rb in range(seq // RB):
            r0 = rb * RB
            acc = xpad_ref[pl.ds(pad + r0 - CONV_LEFT, RB), :] * cw[0:1, :]
            for k in range(1, CONV_W):
                acc = acc + xpad_ref[pl.ds(pad + r0 + k - CONV_LEFT, RB), :] * cw[k:k + 1, :]
            y = acc * (jnp.tanh(acc) + 1.0)
            if normalize:
                y = y * (lax.rsqrt(jnp.sum(y * y, axis=-1, keepdims=True) + 1e-6) * scale)
            dst_ref[r0:r0 + RB, :] = y

    conv_silu(q_ref, cwq_ref, qn_ref, True, DK ** -0.5)
    conv_silu(k_ref, cwk_ref, kn_ref, True, 1.0)
    conv_silu(v_ref, cwv_ref, vv_ref, False, 1.0)

    ab = ab_ref[0]
    g8 = -jnp.exp(alog_ref[0][:, 0:1]) * _softplus(ab + dtb_ref[0][:, 0:1])
    beta8 = _sigmoid(ab)
    lane = lax.broadcasted_iota(jnp.int32, (SUBLANES, seq), 1) % C
    pre = g8
    suf = g8
    sh = 1
    while sh < C:
        pre = pre + jnp.where(lane >= sh, pltpu.roll(pre, sh, axis=1), 0.0)
        suf = suf + jnp.where(lane < C - sh, pltpu.roll(suf, seq - sh, axis=1), 0.0)
        sh *= 2
    tot = pre + suf - g8
    row8 = lax.broadcasted_iota(jnp.int32, (SUBLANES, seq), 0)
    gcs = jnp.where(row8 == 0, pre, suf)
    eg = jnp.exp(gcs)
    ekd = jnp.exp(tot - gcs)
    etot = jnp.exp(tot)
    fill = jnp.zeros((GDN_BLOCK - 5 * SUBLANES, GDN_BLOCK), F32)
    for b in range(n_blk):
        ls = slice(b * GDN_BLOCK, (b + 1) * GDN_BLOCK)
        table = jnp.concatenate([beta8[:, ls], gcs[:, ls], eg[:, ls], ekd[:, ls], etot[:, ls], fill], axis=0)
        cf_ref[b * GDN_BLOCK:(b + 1) * GDN_BLOCK, :] = table.T
        gcrow_ref[b] = gcs[:, ls]

    ri = lax.broadcasted_iota(jnp.int32, (C, C), 0)
    ci = lax.broadcasted_iota(jnp.int32, (C, C), 1)
    masks = (ri >= ci, ri <= ci)
    stricts = (ri > ci, ri < ci)
    eye = (ri == ci).astype(F32)
    base_blk = (ri // GDN_INV_BASE) == (ci // GDN_INV_BASE)
    sub_sizes = []
    s = GDN_INV_BASE
    while s < C:
        sub_sizes.append(s)
        s *= 2

    def take_rows(x, s, odd):
        off = s if odd else 0
        return jnp.concatenate([x[k + off:k + off + s] for k in range(0, x.shape[0], 2 * s)], axis=0)

    def put_rows(xc, s, odd):
        z = jnp.zeros((s, xc.shape[1]), xc.dtype)
        pieces = []
        for k in range(0, xc.shape[0], s):
            pieces += [z, xc[k:k + s]] if odd else [xc[k:k + s], z]
        return jnp.concatenate(pieces, axis=0)

    rh = lax.broadcasted_iota(jnp.int32, (C // 2, C), 0)
    ch_ = lax.broadcasted_iota(jnp.int32, (C // 2, C), 1)

    def couple_mask(s, odd):
        row = (rh // s) * (2 * s) + (s if odd else 0) + rh % s
        return ((row // (2 * s)) == (ch_ // (2 * s))) & ((row // s) != (ch_ // s))

    couple = tuple(tuple(couple_mask(s, odd) for s in sub_sizes) for odd in (True, False))

    def local_part(i):
        chains = []
        for bb in range(GDN_LOCAL_UNROLL):
            for d in range(2):
                b = i * GDN_LOCAL_UNROLL + bb
                if d == 1:
                    b = n_blk - 1 - b
                r0 = _aligned(b * C, C)
                kn = kn_ref[pl.ds(r0, C), :]
                qn = qn_ref[pl.ds(r0, C), :]
                kq = _dot_nt(jnp.concatenate([kn, qn], axis=0).astype(BF16), kn.astype(BF16))
                chains.append(dict(b=b, r0=r0, d=d, kk=kq[:C], qk=kq[C:]))
        yield
        for ch in chains:
            d = ch["d"]
            cf = cf_ref[pl.ds(ch["r0"], C), :]
            gcr = gcrow_ref[ch["b"]]
            beta_c = cf[:, ROW_BETA + d:ROW_BETA + d + 1]
            gc_c = cf[:, ROW_GC + d:ROW_GC + d + 1]
            decay = jnp.where(masks[d], jnp.exp(jnp.where(masks[d], gc_c - gcr[d:d + 1, :], 0.0)), 0.0)
            n_mat = jnp.where(stricts[d], -(beta_c * ch["kk"] * decay), 0.0)
            ch["qkd"] = jnp.where(masks[d], ch["qk"] * decay, 0.0).astype(BF16)
            ch["n"] = n_mat
            n0 = jnp.where(base_blk, n_mat, 0.0)
            ch["t"] = eye + n0
            ch["p16"] = n0.astype(BF16)
        for ch in chains:
            ch["p16"] = _dot(ch["p16"], ch["p16"]).astype(BF16)
        yield
        for ch in chains:
            pt = _dot(ch["p16"], jnp.concatenate([ch["p16"], ch["t"].astype(BF16)], axis=1))
            ch["p16"] = pt[:, :C].astype(BF16)
            ch["t"] = ch["t"] + pt[:, C:]
        yield
        for ch in chains:
            ch["t"] = ch["t"] + _dot(ch["p16"], ch["t"].astype(BF16))
        yield
        for lvl, s in enumerate(sub_sizes):
            for ch in chains:
                odd = ch["d"] == 0
                e16 = jnp.where(couple[ch["d"]][lvl], take_rows(ch["n"], s, odd), 0.0).astype(BF16)
                t16 = ch["t"].astype(BF16)
                ch["et"] = put_rows(_dot(e16, t16).astype(BF16), s, odd)
                ch["th"] = take_rows(t16, s, odd)
            yield
            for ch in chains:
                ch["t"] = ch["t"] + put_rows(_dot(ch["th"], ch["et"]), s, ch["d"] == 0)
            yield
        for ch in chains:
            d = ch["d"]
            rows = pl.ds(ch["r0"], C)
            cf = cf_ref[rows, :]
            kn = kn_ref[rows, :]
            beta_c = cf[:, ROW_BETA + d:ROW_BETA + d + 1]
            eg_c = cf[:, ROW_EG + d:ROW_EG + d + 1]
            rhs = jnp.concatenate([kn * (beta_c * eg_c), vv_ref[rows, :] * beta_c], axis=1).astype(BF16)
            ch["wu"] = _dot(ch["t"].astype(BF16), rhs).astype(BF16)
        yield
        for ch in chains:
            d, b = ch["d"], ch["b"]
            rows = pl.ds(ch["r0"], C)
            cf = cf_ref[rows, :]
            eg_c = cf[:, ROW_EG + d:ROW_EG + d + 1]
            ekd_c = cf[:, ROW_EKD + d:ROW_EKD + d + 1]
            kdt = (kn_ref[rows, :] * ekd_c).T.astype(BF16)
            gb = _dot(kdt, ch["wu"])
            g_ref[d, b] = (-gb[:, :DK]).astype(BF16)
            bm_ref[d, b] = gb[:, DK:]
            qwu = _dot(ch["qkd"], ch["wu"])
            qt_ref[d, rows, :] = (qn_ref[rows, :] * eg_c - qwu[:, :DK]).astype(BF16)
            qu_ref[d, rows, :] = qwu[:, DK:]

    ng = ng_ref[...]

    def state_part(i, states):
        for bb in range(GDN_LOCAL_UNROLL):
            n = i * GDN_LOCAL_UNROLL + bb
            chunks = (n, n_blk - 1 - n)
            s16 = [s.astype(BF16) for s in states]
            for d in range(2):
                r0 = chunks[d] * C
                et = cf_ref[pl.ds(r0, 1), ROW_ETOT + d:ROW_ETOT + d + 1]
                states[d] = states[d] * et + (_dot(g_ref[d, chunks[d]], s16[d]) + bm_ref[d, chunks[d]])
            for d in range(2):
                rows = pl.ds(chunks[d] * C, C)
                qu_ref[d, rows, :] = qu_ref[d, rows, :] + _dot(qt_ref[d, rows, :], s16[d])
            if 2 * n >= n_blk:
                for c in chunks:
                    rows = pl.ds(c * C, C)
                    o = qu_ref[0, rows, :] + qu_ref[1, rows, :]
                    gate = gate_ref[0, rows, :].astype(F32)
                    o_ref[0, rows, :] = (_rms(o, ng) * _silu(gate)).astype(o_ref.dtype)
            yield

    def interleave(main, filler, every):
        filler_live = True
        for k, _ in enumerate(main):
            if filler_live and k % every == every - 1:
                filler_live = next(filler, "done") != "done"
        for _ in filler:
            pass

    n_iter = n_blk // GDN_LOCAL_UNROLL
    for _ in local_part(0):
        pass

    s0 = jnp.zeros((DK, DV), F32)
    states = [s0, s0]
    for i in range(1, n_iter):
        interleave(local_part(i), state_part(i - 1, states), GDN_STATE_EVERY)
    for _ in state_part(n_iter - 1, states):
        pass


def _gdn_core(qkv3, gate3, abt3, alog8, dtb8, conv_w, norm_g):
    B, S, _ = qkv3.shape
    n_blk = S // GDN_BLOCK
    n_chunks = S // GDN_CHUNK

    def col(off):
        return pl.BlockSpec((1, S, DK), lambda b, h: (b, 0, off + h))

    def cw(off):
        return pl.BlockSpec((CONV_W, DK), lambda b, h: (0, off + h))

    par = pl.BlockSpec((1, SUBLANES, LANES), lambda b, h: (h, 0, 0))
    return pl.pallas_call(
        functools.partial(_gdn_kernel, seq=S),
        grid=(B, HC),
        in_specs=[col(0), col(HC), col(2 * HC),
                  pl.BlockSpec((1, S, DV), lambda b, h: (b, 0, h)),
                  pl.BlockSpec((1, SUBLANES, S), lambda b, h: (h, 0, b)),
                  par, par, cw(0), cw(HC), cw(2 * HC),
                  pl.BlockSpec((1, DV), lambda b, h: (0, 0))],
        out_specs=pl.BlockSpec((1, S, DV), lambda b, h: (b, 0, h)),
        out_shape=jax.ShapeDtypeStruct((B, S, DC), BF16),
        scratch_shapes=[pltpu.VMEM((S + 2 * SUBLANES, DK), F32),
                        pltpu.VMEM((S, DK), F32), pltpu.VMEM((S, DK), F32), pltpu.VMEM((S, DV), F32),
                        pltpu.VMEM((S, LANES), F32), pltpu.VMEM((n_blk, SUBLANES, GDN_BLOCK), F32),
                        pltpu.VMEM((2, n_chunks, DK, DK), BF16), pltpu.VMEM((2, n_chunks, DK, DV), F32),
                        pltpu.VMEM((2, S, DK), BF16), pltpu.VMEM((2, S, DV), F32)],
        compiler_params=_params(2),
        name="gdn",
    )(qkv3, qkv3, qkv3, gate3, abt3, alog8, dtb8, conv_w, conv_w, conv_w, norm_g.reshape(1, DV))


def _mixer_gdn(x2, B, S, g, w_in, conv_w, a_log, dt_bias, norm_g):
    w_ab = w_in[:, 4 * DC:].reshape(D_MODEL, 2, 2, HC).transpose(3, 1, 2, 0).reshape(HC, 4, D_MODEL)
    w_ab = jnp.concatenate([w_ab, jnp.zeros_like(w_ab)], axis=1).reshape(HC * SUBLANES, D_MODEL)
    qkv, gate, abt = _inproj_t(x2, g, w_in[:, :4 * DC].astype(BF16), w_ab.astype(BF16),
                               ((0, 3 * DC), (3 * DC, 4 * DC)), (F32, BF16))

    def per_head(p):
        p8 = jnp.concatenate([p.T.astype(F32), jnp.zeros((HC, SUBLANES - 2), F32)], axis=1)
        return jnp.broadcast_to(p8[:, :, None], (HC, SUBLANES, LANES))

    y = _gdn_core(qkv.reshape(B, S, 3 * DC), gate.reshape(B, S, DC), abt.reshape(HC, SUBLANES, B * S),
                  per_head(a_log), per_head(dt_bias), conv_w, norm_g)
    return y.reshape(B * S, DC)


def kernel(x, norm_mix_g, norm_mlp_g, mlp_w_up, mlp_w_down, norm_final_g, a_w_in, a_conv_w, a_conv_b, a_gate_w, a_gate_b, a_lambda, a_w_out, b_w_in, b_ln_g, b_ln_b, b_w_s, b_b_s, b_w_out, c_w_in, c_conv_w, c_a_log, c_dt_bias, c_norm_g, c_w_out):
    B, S, _ = x.shape
    depth = norm_mix_g.shape[0]
    x2 = x.reshape(B * S, D_MODEL)
    for i in range(depth):
        kind, j = i % N_MIXERS, i // N_MIXERS
        g = norm_mix_g[i]
        y, w_out = None, None
        if kind == 0:
            y = _mixer_rglru(x2, B, S, g, a_w_in[j], a_conv_w[j], a_conv_b[j], a_gate_w[j], a_gate_b[j], a_lambda[j])
            w_out = a_w_out[j].astype(BF16)
        elif kind == 1:
            x2 = _sgu(x2, g, b_w_in[j], b_ln_g[j], b_ln_b[j], b_w_s[j], b_b_s[j], b_w_out[j])
        else:
            y = _mixer_gdn(x2, B, S, g, c_w_in[j], c_conv_w[j], c_a_log[j], c_dt_bias[j], c_norm_g[j])
            w_out = c_w_out[j].astype(BF16)
        x2 = _mlp(x2, norm_mlp_g[i], mlp_w_up[i].astype(BF16), mlp_w_down[i].astype(BF16),
                  norm_final_g if i == depth - 1 else None, y, w_out)
    return x2.reshape(B, S, D_MODEL)
```

```python
import functools

import jax
import jax.numpy as jnp
from jax import lax
from jax.experimental import pallas as pl
from jax.experimental.pallas import tpu as pltpu

F32 = jnp.float32
BF16 = jnp.bfloat16

D_MODEL = 1024
D_FF = 4 * D_MODEL
N_MIXERS = 3
LANES = 128
SUBLANES = 8
VMEM_LIMIT = 56 * 1024 * 1024

HA = 8
HDA = D_MODEL // HA
CONV_W = 4
CONV_LEFT = CONV_W // 2
RG_C = 8.0
LOG2E = 1.4426950408889634
RG_SEG = 64
SEG_PAD = 8
RG_ROWS = 256
SGU_CHUNK = 128
GB = 8
DGB = D_MODEL // GB
HC = 8
DK = 128
DV = 128
DC = HC * DV
GDN_CHUNK = LANES
GDN_BLOCK = GDN_CHUNK
GDN_INV_BASE = 8
GDN_LOCAL_UNROLL = 8
GDN_STATE_EVERY = 1


def _params(n_parallel):
    return pltpu.CompilerParams(dimension_semantics=("parallel",) * n_parallel,
                                vmem_limit_bytes=VMEM_LIMIT)


def _rms(xf, g, eps=1e-6):
    return xf * lax.rsqrt(jnp.mean(xf * xf, axis=-1, keepdims=True) + eps) * g


def _sigmoid(x):
    return 0.5 * jnp.tanh(0.5 * x) + 0.5


def _silu(x):
    return x * _sigmoid(x)


def _gelu_tanh(x):
    c = 0.7978845608028654
    return 0.5 * x * (1.0 + jnp.tanh(c * (x + 0.044715 * (x * x * x))))


def _softplus(x):
    return jnp.maximum(x, 0.0) + jnp.log1p(jnp.exp(-jnp.abs(x)))


def _dot(a, b):
    return jnp.dot(a, b, preferred_element_type=F32)


def _aligned(offset, multiple):
    return offset if isinstance(offset, int) else pl.multiple_of(offset, multiple)


def _dot_nt(a, b):
    return lax.dot_general(a, b, (((1,), (1,)), ((), ())), preferred_element_type=F32)


MLP_TM = 512
MLP_FC = 1024


def _mlp_kernel(x_ref, g_ref, wup_ref, wdn_ref, *rest, final_norm, mixer_proj):
    rest = list(rest)
    o_ref = rest.pop()
    x = x_ref[...]
    if mixer_proj:
        y_ref, wo_ref = rest[:2]
        rest = rest[2:]
        x = x + _dot(y_ref[...], wo_ref[...])
    if final_norm:
        (gf_ref,) = rest
    h = _rms(x, g_ref[...]).astype(BF16)
    acc = x
    for c in range(D_FF // MLP_FC):
        u = _dot(h, wup_ref[:, c * MLP_FC:(c + 1) * MLP_FC])
        a = jnp.square(jnp.maximum(u, 0.0)).astype(BF16)
        acc = acc + _dot(a, wdn_ref[c * MLP_FC:(c + 1) * MLP_FC, :])
    if final_norm:
        acc = _rms(acc, gf_ref[...])
    o_ref[...] = acc


def _const_spec(shape):
    n = len(shape)
    return pl.BlockSpec(shape, lambda *_: (0,) * n, pipeline_mode=pl.Buffered(1))


def _mlp(x2, g, w_up, w_down, g_final=None, mixer_y=None, mixer_w_out=None):
    T = x2.shape[0]
    tok = pl.BlockSpec((MLP_TM, D_MODEL), lambda i: (i, 0))
    in_specs = [tok, _const_spec((1, D_MODEL)), _const_spec((D_MODEL, D_FF)), _const_spec((D_FF, D_MODEL))]
    args = [x2, g.reshape(1, D_MODEL), w_up, w_down]
    if mixer_y is not None:
        in_specs += [tok, _const_spec((D_MODEL, D_MODEL))]
        args += [mixer_y, mixer_w_out]
    if g_final is not None:
        in_specs.append(_const_spec((1, D_MODEL)))
        args.append(g_final.reshape(1, D_MODEL))
    return pl.pallas_call(
        functools.partial(_mlp_kernel, final_norm=g_final is not None, mixer_proj=mixer_y is not None),
        grid=(T // MLP_TM,),
        in_specs=in_specs,
        out_specs=tok,
        out_shape=jax.ShapeDtypeStruct((T, D_MODEL), F32),
        compiler_params=_params(1),
        name="mlp",
    )(*args)


PROJ_TM = 512


def _inproj_kernel(x_ref, g_ref, w_ref, *o_refs, splits):
    h = _rms(x_ref[...], g_ref[...]).astype(BF16)
    for o_ref, (c0, c1) in zip(o_refs, splits):
        o_ref[...] = _dot(h, w_ref[:, c0:c1]).astype(o_ref.dtype)


def _inproj(x2, g, w, splits, dtypes):
    T = x2.shape[0]
    n_out = w.shape[1]
    tok = pl.BlockSpec((PROJ_TM, D_MODEL), lambda i: (i, 0))
    return pl.pallas_call(
        functools.partial(_inproj_kernel, splits=splits),
        grid=(T // PROJ_TM,),
        in_specs=[tok, _const_spec((1, D_MODEL)), _const_spec((D_MODEL, n_out))],
        out_specs=[pl.BlockSpec((PROJ_TM, c1 - c0), lambda i: (i, 0)) for c0, c1 in splits],
        out_shape=[jax.ShapeDtypeStruct((T, c1 - c0), dt) for (c0, c1), dt in zip(splits, dtypes)],
        compiler_params=_params(1),
        name="inproj",
    )(x2, g.reshape(1, D_MODEL), w)


SGU_TM = 512
SGU_PARTS = 2


def _sgu_kernel(x_ref, g_ref, win_ref, lng_ref, lnb_ref, ws_ref, bs_ref, wout_ref, o_ref, y_ref):
    part_rows = SGU_TM // SGU_PARTS
    n_chunks = part_rows // SGU_CHUNK
    parts = [dict(r0=p * part_rows) for p in range(SGU_PARTS)]
    for pt in parts:
        pt["x"] = x_ref[pt["r0"]:pt["r0"] + part_rows, :]
        pt["h"] = _rms(pt["x"], g_ref[...]).astype(BF16)
    for pt in parts:
        pt["u"] = _gelu_tanh(_dot(pt["h"], win_ref[:, :D_MODEL]))
    for pt in parts:
        v = _gelu_tanh(_dot(pt["h"], win_ref[:, D_MODEL:]))
        mu = jnp.mean(v, axis=-1, keepdims=True)
        vc = v - mu
        var = jnp.mean(vc * vc, axis=-1, keepdims=True)
        pt["v"] = (vc * lax.rsqrt(var + 1e-5) * lng_ref[...] + lnb_ref[...]).astype(BF16)
    for pt in parts:
        u, v = pt["u"], pt["v"]
        for gi in range(GB):
            cols = slice(gi * DGB, (gi + 1) * DGB)
            vg = jnp.concatenate([v[c * SGU_CHUNK:(c + 1) * SGU_CHUNK, cols] for c in range(n_chunks)], axis=1)
            vs = _dot(ws_ref[gi], vg)
            for c in range(n_chunks):
                rows = slice(c * SGU_CHUNK, (c + 1) * SGU_CHUNK)
                vsc = vs[:, c * DGB:(c + 1) * DGB] + bs_ref[gi]
                y_ref[pt["r0"] + c * SGU_CHUNK:pt["r0"] + (c + 1) * SGU_CHUNK, cols] = (u[rows, cols] * vsc).astype(BF16)
    for pt in parts:
        rows = slice(pt["r0"], pt["r0"] + part_rows)
        o_ref[rows, :] = pt["x"] + _dot(y_ref[rows, :], wout_ref[...])


def _sgu(x2, g, w_in, ln_g, ln_b, w_s, b_s, w_out):
    T = x2.shape[0]
    tok = pl.BlockSpec((SGU_TM, D_MODEL), lambda i: (i, 0))
    bs = jnp.broadcast_to(b_s[:, :, None], (GB, SGU_CHUNK, DGB)).astype(F32)
    return pl.pallas_call(
        _sgu_kernel,
        grid=(T // SGU_TM,),
        in_specs=[tok, _const_spec((1, D_MODEL)), _const_spec((D_MODEL, 2 * D_MODEL)),
                  _const_spec((1, D_MODEL)), _const_spec((1, D_MODEL)),
                  _const_spec((GB, SGU_CHUNK, SGU_CHUNK)), _const_spec((GB, SGU_CHUNK, DGB)),
                  _const_spec((D_MODEL, D_MODEL))],
        out_specs=tok,
        out_shape=jax.ShapeDtypeStruct((T, D_MODEL), F32),
        scratch_shapes=[pltpu.VMEM((SGU_TM, D_MODEL), BF16)],
        compiler_params=_params(1),
        name="sgu",
    )(x2, g.reshape(1, D_MODEL), w_in.astype(BF16), ln_g.reshape(1, D_MODEL), ln_b.reshape(1, D_MODEL),
      w_s.astype(BF16), bs, w_out.astype(BF16))


def _rglru_kernel(x_ref, g_ref, win_ref, cw_ref, cb_ref, gw_ref, gb_ref, lam_ref, o_ref,
                  hn_ref, gate_ref, xs_ref, xq_ref, a_ref, b_ref, h_ref, p_ref, ys_ref, *, seq):
    @pl.when(pl.program_id(1) == 0)
    def _():
        for rb in range(seq // RG_ROWS):
            rows = slice(rb * RG_ROWS, (rb + 1) * RG_ROWS)
            hn_ref[rows, :] = _rms(x_ref[0, rows, :], g_ref[...]).astype(BF16)

    seg = RG_SEG
    pitch = seg + SEG_PAD
    n_seg = seq // seg
    n_grp = n_seg // SUBLANES
    halo_lo, halo_hi = CONV_LEFT, CONV_W - 1 - CONV_LEFT
    grp_rows = (seg + halo_lo + halo_hi) * SUBLANES
    row = lax.broadcasted_iota(jnp.int32, (SUBLANES, HDA), 0)

    grp_tokens = SUBLANES * seg

    def project(g):
        rows = slice(g * grp_tokens, (g + 1) * grp_tokens)
        z = _dot(hn_ref[rows, :], win_ref[0])
        gate_ref[rows, :] = z[:, :HDA]
        for s in range(SUBLANES):
            q = g * SUBLANES + s
            xs_ref[q * pitch:q * pitch + seg, :] = z[s * seg:(s + 1) * seg, HDA:]

    def to_scan_order(g):
        for i in range(seg):
            r0 = g * grp_rows + (halo_lo + i) * SUBLANES
            xq_ref[r0:r0 + SUBLANES, :] = xs_ref[pl.ds(g * SUBLANES * pitch + i, SUBLANES, stride=pitch), :]

    def xq_step(g, i):
        r0 = g * grp_rows + (halo_lo + i) * SUBLANES
        return xq_ref[r0:r0 + SUBLANES, :]

    zrow = jnp.zeros((1, HDA), F32)

    def conv_halo(g):
        for i in range(seg - halo_lo, seg):
            edge = xq_step(g - 1, i)[SUBLANES - 1:SUBLANES, :] if g > 0 else zrow
            v = jnp.where(row == 0, edge, pltpu.roll(xq_step(g, i), 1, axis=0))
            r0 = g * grp_rows + (halo_lo + i - seg) * SUBLANES
            xq_ref[r0:r0 + SUBLANES, :] = v
        for i in range(halo_hi):
            edge = xq_step(g + 1, i)[0:1, :] if g < n_grp - 1 else zrow
            v = jnp.where(row == SUBLANES - 1, edge, pltpu.roll(xq_step(g, i), SUBLANES - 1, axis=0))
            r0 = g * grp_rows + (halo_lo + seg + i) * SUBLANES
            xq_ref[r0:r0 + SUBLANES, :] = v

    cw = 0.5 * cw_ref[...]
    cb = 0.5 * cb_ref[...]
    gw = gw_ref[0]
    gbias = gb_ref[0]
    hn = [(-0.5 * RG_C * LOG2E) * _softplus(-lam_ref[d:d + 1, :]) for d in range(2)]

    steps_per_blk = RG_ROWS // SUBLANES

    def gates_and_inputs(g):
        for ib in range(seg // steps_per_blk):
            src = g * grp_rows + ib * RG_ROWS
            dst = (g * seg + ib * steps_per_blk) * SUBLANES
            xh = cb
            for k in range(CONV_W):
                xh = xh + xq_ref[src + k * SUBLANES:src + k * SUBLANES + RG_ROWS, :] * cw[k:k + 1, :]
            gates = jnp.tanh(_dot(xh.astype(BF16), gw) + gbias)
            for d in range(2):
                t_r = gates[:, (2 * d) * HDA:(2 * d + 1) * HDA]
                t_i = gates[:, (2 * d + 1) * HDA:(2 * d + 2) * HDA]
                a = jnp.exp2(t_r * hn[d] + hn[d])
                y = 1.0 - a * a
                a_ref[d, dst:dst + RG_ROWS, :] = a
                b_ref[d, dst:dst + RG_ROWS, :] = (y * lax.rsqrt(jnp.maximum(y, 1e-30))) * ((t_i + 1.0) * xh)

    for g in range(min(2, n_grp)):
        project(g)
        to_scan_order(g)
    for g in range(n_grp):
        if g + 2 < n_grp:
            project(g + 2)
        conv_halo(g)
        gates_and_inputs(g)
        if g + 2 < n_grp:
            to_scan_order(g + 2)

    def scan_step(i, carry):
        out = []
        row_f = _aligned(i * SUBLANES, SUBLANES)
        row_r = _aligned((seg - 1 - i) * SUBLANES, SUBLANES)
        for g in range(n_grp):
            hf, pf, hb, pb = carry[4 * g:4 * g + 4]
            fwd = pl.ds(row_f + g * seg * SUBLANES, SUBLANES)
            rev = pl.ds(row_r + g * seg * SUBLANES, SUBLANES)
            af = a_ref[0, fwd, :]
            ab = a_ref[1, rev, :]
            hf = af * hf + b_ref[0, fwd, :]
            pf = af * pf
            hb = ab * hb + b_ref[1, rev, :]
            pb = ab * pb
            h_ref[0, fwd, :] = hf
            p_ref[0, fwd, :] = pf
            h_ref[1, rev, :] = hb
            p_ref[1, rev, :] = pb
            out += [hf, pf, hb, pb]
        return tuple(out)

    z = jnp.zeros((SUBLANES, HDA), F32)
    o = jnp.ones((SUBLANES, HDA), F32)
    ends = lax.fori_loop(0, seg, scan_step, (z, o, z, o) * n_grp, unroll=2)

    cf = [None] * n_grp
    cr = [None] * n_grp
    c_in = jnp.zeros((1, HDA), F32)
    for g in range(n_grp):
        hf, pf = ends[4 * g], ends[4 * g + 1]
        c = jnp.broadcast_to(c_in, (SUBLANES, HDA))
        for _ in range(SUBLANES - 1):
            c = jnp.where(row == 0, c_in, pltpu.roll(hf + pf * c, 1, axis=0))
        cf[g] = c
        c_in = (hf + pf * c)[SUBLANES - 1:SUBLANES, :]
    c_in = jnp.zeros((1, HDA), F32)
    for g in reversed(range(n_grp)):
        hb, pb = ends[4 * g + 2], ends[4 * g + 3]
        c = jnp.broadcast_to(c_in, (SUBLANES, HDA))
        for _ in range(SUBLANES - 1):
            c = jnp.where(row == SUBLANES - 1, c_in, pltpu.roll(hb + pb * c, SUBLANES - 1, axis=0))
        cr[g] = c
        c_in = (hb + pb * c)[0:1, :]

    for g in range(n_grp):
        cfb = jnp.tile(cf[g], (steps_per_blk, 1))
        crb = jnp.tile(cr[g], (steps_per_blk, 1))
        for ib in range(seg // steps_per_blk):
            rows = slice((g * seg + ib * steps_per_blk) * SUBLANES, (g * seg + (ib + 1) * steps_per_blk) * SUBLANES)
            ys_ref[rows, :] = ((h_ref[0, rows, :] + p_ref[0, rows, :] * cfb)
                               + (h_ref[1, rows, :] + p_ref[1, rows, :] * crb))

    for q in range(n_seg):
        g, s = divmod(q, SUBLANES)
        y = jnp.concatenate(
            [ys_ref[pl.ds((g * seg + i0) * SUBLANES + s, SUBLANES, stride=SUBLANES), :]
             for i0 in range(0, seg, SUBLANES)], axis=0)
        gate = gate_ref[q * seg:(q + 1) * seg, :]
        o_ref[0, q * seg:(q + 1) * seg, :] = (y * _gelu_tanh(gate)).astype(o_ref.dtype)


def _mixer_rglru(x2, B, S, g, w_in, conv_w, conv_b, gate_w, gate_b, lam):
    n_seg = S // RG_SEG
    xs_rows = n_seg * (RG_SEG + SEG_PAD)
    xq_rows = (n_seg // SUBLANES) * (RG_SEG + CONV_W - 1) * SUBLANES
    w2 = jnp.concatenate([w_in[:, :D_MODEL].reshape(D_MODEL, HA, HDA),
                          w_in[:, D_MODEL:].reshape(D_MODEL, HA, HDA)], axis=2).transpose(1, 0, 2).astype(BF16)
    gw = gate_w.transpose(2, 3, 0, 1, 4).reshape(HA, HDA, 4 * HDA).astype(BF16)
    gb = (0.5 * gate_b).transpose(2, 0, 1, 3).reshape(HA, 1, 4 * HDA)
    y = pl.pallas_call(
        functools.partial(_rglru_kernel, seq=S),
        grid=(B, HA),
        in_specs=[pl.BlockSpec((1, S, D_MODEL), lambda b, h: (b, 0, 0)),
                  pl.BlockSpec((1, D_MODEL), lambda b, h: (0, 0)),
                  pl.BlockSpec((1, D_MODEL, 2 * HDA), lambda b, h: (h, 0, 0)),
                  pl.BlockSpec((CONV_W, HDA), lambda b, h: (0, h)),
                  pl.BlockSpec((1, HDA), lambda b, h: (0, h)),
                  pl.BlockSpec((1, HDA, 4 * HDA), lambda b, h: (h, 0, 0)),
                  pl.BlockSpec((1, 1, 4 * HDA), lambda b, h: (h, 0, 0)),
                  pl.BlockSpec((2, HDA), lambda b, h: (0, h))],
        out_specs=pl.BlockSpec((1, S, HDA), lambda b, h: (b, 0, h)),
        out_shape=jax.ShapeDtypeStruct((B, S, D_MODEL), BF16),
        scratch_shapes=[pltpu.VMEM((S, D_MODEL), BF16), pltpu.VMEM((S, HDA), F32),
                        pltpu.VMEM((xs_rows, HDA), F32), pltpu.VMEM((xq_rows, HDA), F32),
                        pltpu.VMEM((2, S, HDA), F32), pltpu.VMEM((2, S, HDA), F32),
                        pltpu.VMEM((2, S, HDA), F32), pltpu.VMEM((2, S, HDA), F32),
                        pltpu.VMEM((S, HDA), F32)],
        compiler_params=pltpu.CompilerParams(dimension_semantics=("parallel", "arbitrary"),
                                             vmem_limit_bytes=VMEM_LIMIT),
        name="rglru",
    )(x2.reshape(B, S, D_MODEL), g.reshape(1, D_MODEL), w2, conv_w, conv_b.reshape(1, D_MODEL), gw, gb, lam)
    return y.reshape(B * S, D_MODEL)


ROW_BETA, ROW_GC, ROW_EG, ROW_EKD, ROW_ETOT = 2, 8, 16, 24, 32


def _inproj_t_kernel(x_ref, g_ref, w_ref, wt_ref, *o_refs, splits):
    h = _rms(x_ref[...], g_ref[...]).astype(BF16)
    for o_ref, (c0, c1) in zip(o_refs[:-1], splits):
        o_ref[...] = _dot(h, w_ref[:, c0:c1]).astype(o_ref.dtype)
    o_refs[-1][...] = _dot_nt(wt_ref[...], h)


def _inproj_t(x2, g, w, wt, splits, dtypes):
    T = x2.shape[0]
    n_out = w.shape[1]
    n_t = wt.shape[0]
    tok = pl.BlockSpec((PROJ_TM, D_MODEL), lambda i: (i, 0))
    return pl.pallas_call(
        functools.partial(_inproj_t_kernel, splits=splits),
        grid=(T // PROJ_TM,),
        in_specs=[tok, _const_spec((1, D_MODEL)), _const_spec((D_MODEL, n_out)), _const_spec((n_t, D_MODEL))],
        out_specs=[pl.BlockSpec((PROJ_TM, c1 - c0), lambda i: (i, 0)) for c0, c1 in splits]
        + [pl.BlockSpec((n_t, PROJ_TM), lambda i: (0, i))],
        out_shape=[jax.ShapeDtypeStruct((T, c1 - c0), dt) for (c0, c1), dt in zip(splits, dtypes)]
        + [jax.ShapeDtypeStruct((n_t, T), F32)],
        compiler_params=_params(1),
        name="inproj_t",
    )(x2, g.reshape(1, D_MODEL), w, wt)


def _gdn_kernel(q_ref, k_ref, v_ref, gate_ref, ab_ref, alog_ref, dtb_ref, cwq_ref, cwk_ref, cwv_ref, ng_ref,
                o_ref,
                xpad_ref, qn_ref, kn_ref, vv_ref, cf_ref, gcrow_ref, g_ref, bm_ref, qt_ref, qu_ref,
                *, seq):
    pad = SUBLANES
    n_blk = seq // GDN_BLOCK
    C = GDN_CHUNK
    RB = 256

    zeros = jnp.zeros((pad, DK), F32)
    xpad_ref[0:pad, :] = zeros
    xpad_ref[pad + seq:pad + seq + pad, :] = zeros

    def conv_silu(src_ref, cw_ref, dst_ref, normalize, scale):
        xpad_ref[pad:pad + seq, :] = src_ref[0]
        cw = 0.5 * cw_ref[...]
        for rb in range(seq // RB):
            r0 = rb * RB
            acc = xpad_ref[pl.ds(pad + r0 - CONV_LEFT, RB), :] * cw[0:1, :]
            for k in range(1, CONV_W):
                acc = acc + xpad_ref[pl.ds(pad + r0 + k - CONV_LEFT, RB), :] * cw[k:k + 1, :]
            y = acc * (jnp.tanh(acc) + 1.0)
            if normalize:
                y = y * (lax.rsqrt(jnp.sum(y * y, axis=-1, keepdims=True) + 1e-6) * scale)
            dst_ref[r0:r0 + RB, :] = y

    conv_silu(q_ref, cwq_ref, qn_ref, True, DK ** -0.5)
    conv_silu(k_ref, cwk_ref, kn_ref, True, 1.0)
    conv_silu(v_ref, cwv_ref, vv_ref, False, 1.0)

    ab = ab_ref[0]
    g8 = -jnp.exp(alog_ref[0][:, 0:1]) * _softplus(ab + dtb_ref[0][:, 0:1])
    beta8 = _sigmoid(ab)
    lane = lax.broadcasted_iota(jnp.int32, (SUBLANES, seq), 1) % C
    pre = g8
    suf = g8
    sh = 1
    while sh < C:
        pre = pre + jnp.where(lane >= sh, pltpu.roll(pre, sh, axis=1), 0.0)
        suf = suf + jnp.where(lane < C - sh, pltpu.roll(suf, seq - sh, axis=1), 0.0)
        sh *= 2
    tot = pre + suf - g8
    row8 = lax.broadcasted_iota(jnp.int32, (SUBLANES, seq), 0)
    gcs = jnp.where(row8 == 0, pre, suf)
    eg = jnp.exp(gcs)
    ekd = jnp.exp(tot - gcs)
    etot = jnp.exp(tot)
    fill = jnp.zeros((GDN_BLOCK - 5 * SUBLANES, GDN_BLOCK), F32)
    for b in range(n_blk):
        ls = slice(b * GDN_BLOCK, (b + 1) * GDN_BLOCK)
        table = jnp.concatenate([beta8[:, ls], gcs[:, ls], eg[:, ls], ekd[:, ls], etot[:, ls], fill], axis=0)
        cf_ref[b * GDN_BLOCK:(b + 1) * GDN_BLOCK, :] = table.T
        gcrow_ref[b] = gcs[:, ls]

    ri = lax.broadcasted_iota(jnp.int32, (C, C), 0)
    ci = lax.broadcasted_iota(jnp.int32, (C, C), 1)
    masks = (ri >= ci, ri <= ci)
    stricts = (ri > ci, ri < ci)
    eye = (ri == ci).astype(F32)
    base_blk = (ri // GDN_INV_BASE) == (ci // GDN_INV_BASE)
    sub_sizes = []
    s = GDN_INV_BASE
    while s < C:
        sub_sizes.append(s)
        s *= 2

    def take_rows(x, s, odd):
        off = s if odd else 0
        return jnp.concatenate([x[k + off:k + off + s] for k in range(0, x.shape[0], 2 * s)], axis=0)

    def put_rows(xc, s, odd):
        z = jnp.zeros((s, xc.shape[1]), xc.dtype)
        pieces = []
        for k in range(0, xc.shape[0], s):
            pieces += [z, xc[k:k + s]] if odd else [xc[k:k + s], z]
        return jnp.concatenate(pieces, axis=0)

    rh = lax.broadcasted_iota(jnp.int32, (C // 2, C), 0)
    ch_ = lax.broadcasted_iota(jnp.int32, (C // 2, C), 1)

    def couple_mask(s, odd):
        row = (rh // s) * (2 * s) + (s if odd else 0) + rh % s
        return ((row // (2 * s)) == (ch_ // (2 * s))) & ((row // s) != (ch_ // s))

    couple = tuple(tuple(couple_mask(s, odd) for s in sub_sizes) for odd in (True, False))

    def local_part(i):
        chains = []
        for bb in range(GDN_LOCAL_UNROLL):
            for d in range(2):
                b = i * GDN_LOCAL_UNROLL + bb
                if d == 1:
                    b = n_blk - 1 - b
                r0 = _aligned(b * C, C)
                kn = kn_ref[pl.ds(r0, C), :]
                qn = qn_ref[pl.ds(r0, C), :]
                kq = _dot_nt(jnp.concatenate([kn, qn], axis=0).astype(BF16), kn.astype(BF16))
                chains.append(dict(b=b, r0=r0, d=d, kk=kq[:C], qk=kq[C:]))
        yield
        for ch in chains:
            d = ch["d"]
            cf = cf_ref[pl.ds(ch["r0"], C), :]
            gcr = gcrow_ref[ch["b"]]
            beta_c = cf[:, ROW_BETA + d:ROW_BETA + d + 1]
            gc_c = cf[:, ROW_GC + d:ROW_GC + d + 1]
            decay = jnp.where(masks[d], jnp.exp(jnp.where(masks[d], gc_c - gcr[d:d + 1, :], 0.0)), 0.0)
            n_mat = jnp.where(stricts[d], -(beta_c * ch["kk"] * decay), 0.0)
            ch["qkd"] = jnp.where(masks[d], ch["qk"] * decay, 0.0).astype(BF16)
            ch["n"] = n_mat
            n0 = jnp.where(base_blk, n_mat, 0.0)
            ch["t"] = eye + n0
            ch["p16"] = n0.astype(BF16)
        for ch in chains:
            ch["p16"] = _dot(ch["p16"], ch["p16"]).astype(BF16)
        yield
        for ch in chains:
            pt = _dot(ch["p16"], jnp.concatenate([ch["p16"], ch["t"].astype(BF16)], axis=1))
            ch["p16"] = pt[:, :C].astype(BF16)
            ch["t"] = ch["t"] + pt[:, C:]
        yield
        for ch in chains:
            ch["t"] = ch["t"] + _dot(ch["p16"], ch["t"].astype(BF16))
        yield
        for lvl, s in enumerate(sub_sizes):
            for ch in chains:
                odd = ch["d"] == 0
                e16 = jnp.where(couple[ch["d"]][lvl], take_rows(ch["n"], s, odd), 0.0).astype(BF16)
                t16 = ch["t"].astype(BF16)
                ch["et"] = put_rows(_dot(e16, t16).astype(BF16), s, odd)
                ch["th"] = take_rows(t16, s, odd)
            yield
            for ch in chains:
                ch["t"] = ch["t"] + put_rows(_dot(ch["th"], ch["et"]), s, ch["d"] == 0)
            yield
        for ch in chains:
            d = ch["d"]
            rows = pl.ds(ch["r0"], C)
            cf = cf_ref[rows, :]
            kn = kn_ref[rows, :]
            beta_c = cf[:, ROW_BETA + d:ROW_BETA + d + 1]
            eg_c = cf[:, ROW_EG + d:ROW_EG + d + 1]
            rhs = jnp.concatenate([kn * (beta_c * eg_c), vv_ref[rows, :] * beta_c], axis=1).astype(BF16)
            ch["wu"] = _dot(ch["t"].astype(BF16), rhs).astype(BF16)
        yield
        for ch in chains:
            d, b = ch["d"], ch["b"]
            rows = pl.ds(ch["r0"], C)
            cf = cf_ref[rows, :]
            eg_c = cf[:, ROW_EG + d:ROW_EG + d + 1]
            ekd_c = cf[:, ROW_EKD + d:ROW_EKD + d + 1]
            kdt = (kn_ref[rows, :] * ekd_c).T.astype(BF16)
            gb = _dot(kdt, ch["wu"])
            g_ref[d, b] = (-gb[:, :DK]).astype(BF16)
            bm_ref[d, b] = gb[:, DK:]
            qwu = _dot(ch["qkd"], ch["wu"])
            qt_ref[d, rows, :] = (qn_ref[rows, :] * eg_c - qwu[:, :DK]).astype(BF16)
            qu_ref[d, rows, :] = qwu[:, DK:]

    ng = ng_ref[...]

    def state_part(i, states):
        for bb in range(GDN_LOCAL_UNROLL):
            n = i * GDN_LOCAL_UNROLL + bb
            chunks = (n, n_blk - 1 - n)
            s16 = [s.astype(BF16) for s in states]
            for d in range(2):
                r0 = chunks[d] * C
                et = cf_ref[pl.ds(r0, 1), ROW_ETOT + d:ROW_ETOT + d + 1]
                states[d] = states[d] * et + (_dot(g_ref[d, chunks[d]], s16[d]) + bm_ref[d, chunks[d]])
            for d in range(2):
                rows = pl.ds(chunks[d] * C, C)
                qu_ref[d, rows, :] = qu_ref[d, rows, :] + _dot(qt_ref[d, rows, :], s16[d])
            if 2 * n >= n_blk:
                for c in chunks:
                    rows = pl.ds(c * C, C)
                    o = qu_ref[0, rows, :] + qu_ref[1, rows, :]
                    gate = gate_ref[0, rows, :].astype(F32)
                    o_ref[0, rows, :] = (_rms(o, ng) * _silu(gate)).astype(o_ref.dtype)
            yield

    def interleave(main, filler, every):
        filler_live = True
        for k, _ in enumerate(main):
            if filler_live and k % every == every - 1:
                filler_live = next(filler, "done") != "done"
        for _ in filler:
            pass

    n_iter = n_blk // GDN_LOCAL_UNROLL
    for _ in local_part(0):
        pass

    s0 = jnp.zeros((DK, DV), F32)
    states = [s0, s0]
    for i in range(1, n_iter):
        interleave(local_part(i), state_part(i - 1, states), GDN_STATE_EVERY)
    for _ in state_part(n_iter - 1, states):
        pass


def _gdn_core(qkv3, gate3, abt3, alog8, dtb8, conv_w, norm_g):
    B, S, _ = qkv3.shape
    n_blk = S // GDN_BLOCK
    n_chunks = S // GDN_CHUNK

    def col(off):
        return pl.BlockSpec((1, S, DK), lambda b, h: (b, 0, off + h))

    def cw(off):
        return pl.BlockSpec((CONV_W, DK), lambda b, h: (0, off + h))

    par = pl.BlockSpec((1, SUBLANES, LANES), lambda b, h: (h, 0, 0))
    return pl.pallas_call(
        functools.partial(_gdn_kernel, seq=S),
        grid=(B, HC),
        in_specs=[col(0), col(HC), col(2 * HC),
                  pl.BlockSpec((1, S, DV), lambda b, h: (b, 0, h)),
                  pl.BlockSpec((1, SUBLANES, S), lambda b, h: (h, 0, b)),
                  par, par, cw(0), cw(HC), cw(2 * HC),
                  pl.BlockSpec((1, DV), lambda b, h: (0, 0))],
        out_specs=pl.BlockSpec((1, S, DV), lambda b, h: (b, 0, h)),
        out_shape=jax.ShapeDtypeStruct((B, S, DC), BF16),
        scratch_shapes=[pltpu.VMEM((S + 2 * SUBLANES, DK), F32),
                        pltpu.VMEM((S, DK), F32), pltpu.VMEM((S, DK), F32), pltpu.VMEM((S, DV), F32),
                        pltpu.VMEM((S, LANES), F32), pltpu.VMEM((n_blk, SUBLANES, GDN_BLOCK), F32),
                        pltpu.VMEM((2, n_chunks, DK, DK), BF16), pltpu.VMEM((2, n_chunks, DK, DV), F32),
                        pltpu.VMEM((2, S, DK), BF16), pltpu.VMEM((2, S, DV), F32)],
        compiler_params=_params(2),
        name="gdn",
    )(qkv3, qkv3, qkv3, gate3, abt3, alog8, dtb8, conv_w, conv_w, conv_w, norm_g.reshape(1, DV))


def _mixer_gdn(x2, B, S, g, w_in, conv_w, a_log, dt_bias, norm_g):
    w_ab = w_in[:, 4 * DC:].reshape(D_MODEL, 2, 2, HC).transpose(3, 1, 2, 0).reshape(HC, 4, D_MODEL)
    w_ab = jnp.concatenate([w_ab, jnp.zeros_like(w_ab)], axis=1).reshape(HC * SUBLANES, D_MODEL)
    qkv, gate, abt = _inproj_t(x2, g, w_in[:, :4 * DC].astype(BF16), w_ab.astype(BF16),
                               ((0, 3 * DC), (3 * DC, 4 * DC)), (F32, BF16))

    def per_head(p):
        p8 = jnp.concatenate([p.T.astype(F32), jnp.zeros((HC, SUBLANES - 2), F32)], axis=1)
        return jnp.broadcast_to(p8[:, :, None], (HC, SUBLANES, LANES))

    y = _gdn_core(qkv.reshape(B, S, 3 * DC), gate.reshape(B, S, DC), abt.reshape(HC, SUBLANES, B * S),
                  per_head(a_log), per_head(dt_bias), conv_w, norm_g)
    return y.reshape(B * S, DC)


def kernel(x, norm_mix_g, norm_mlp_g, mlp_w_up, mlp_w_down, norm_final_g, a_w_in, a_conv_w, a_conv_b, a_gate_w, a_gate_b, a_lambda, a_w_out, b_w_in, b_ln_g, b_ln_b, b_w_s, b_b_s, b_w_out, c_w_in, c_conv_w, c_a_log, c_dt_bias, c_norm_g, c_w_out):
    B, S, _ = x.shape
    depth = norm_mix_g.shape[0]
    x2 = x.reshape(B * S, D_MODEL)
    for i in range(depth):
        kind, j = i % N_MIXERS, i // N_MIXERS
        g = norm_mix_g[i]
        y, w_out = None, None
        if kind == 0:
            y = _mixer_rglru(x2, B, S, g, a_w_in[j], a_conv_w[j], a_conv_b[j], a_gate_w[j], a_gate_b[j], a_lambda[j])
            w_out = a_w_out[j].astype(BF16)
        elif kind == 1:
            x2 = _sgu(x2, g, b_w_in[j], b_ln_g[j], b_ln_b[j], b_w_s[j], b_b_s[j], b_w_out[j])
        else:
            y = _mixer_gdn(x2, B, S, g, c_w_in[j], c_conv_w[j], c_a_log[j], c_dt_bias[j], c_norm_g[j])
            w_out = c_w_out[j].astype(BF16)
        x2 = _mlp(x2, norm_mlp_g[i], mlp_w_up[i].astype(BF16), mlp_w_down[i].astype(BF16),
                  norm_final_g if i == depth - 1 else None, y, w_out)
    return x2.reshape(B, S, D_MODEL)
```

```python
import functools

import jax
import jax.numpy as jnp
from jax import lax
from jax.experimental import pallas as pl
from jax.experimental.pallas import tpu as pltpu

F32 = jnp.float32
BF16 = jnp.bfloat16

D_MODEL = 1024
D_FF = 4 * D_MODEL
N_MIXERS = 3
LANES = 128
SUBLANES = 8
VMEM_LIMIT = 56 * 1024 * 1024

HA = 8
HDA = D_MODEL // HA
CONV_W = 4
CONV_LEFT = CONV_W // 2
RG_C = 8.0
LOG2E = 1.4426950408889634
RG_SEG = 64
SEG_PAD = 8
RG_ROWS = 256
SGU_CHUNK = 128
GB = 8
DGB = D_MODEL // GB
HC = 8
DK = 128
DV = 128
DC = HC * DV
GDN_CHUNK = LANES
GDN_BLOCK = GDN_CHUNK
GDN_INV_BASE = 8
GDN_LOCAL_UNROLL = 8
GDN_STATE_EVERY = 1


def _params(n_parallel):
    return pltpu.CompilerParams(dimension_semantics=("parallel",) * n_parallel,
                                vmem_limit_bytes=VMEM_LIMIT)


def _rms(xf, g, eps=1e-6):
    return xf * lax.rsqrt(jnp.mean(xf * xf, axis=-1, keepdims=True) + eps) * g


def _sigmoid(x):
    return 0.5 * jnp.tanh(0.5 * x) + 0.5


def _silu(x):
    return x * _sigmoid(x)


def _gelu_tanh(x):
    c = 0.7978845608028654
    return 0.5 * x * (1.0 + jnp.tanh(c * (x + 0.044715 * (x * x * x))))


def _softplus(x):
    return jnp.maximum(x, 0.0) + jnp.log1p(jnp.exp(-jnp.abs(x)))


def _dot(a, b):
    return jnp.dot(a, b, preferred_element_type=F32)


def _aligned(offset, multiple):
    return offset if isinstance(offset, int) else pl.multiple_of(offset, multiple)


def _dot_nt(a, b):
    return lax.dot_general(a, b, (((1,), (1,)), ((), ())), preferred_element_type=F32)


MLP_TM = 512
MLP_FC = 1024


def _mlp_kernel(x_ref, g_ref, wup_ref, wdn_ref, *rest, final_norm, mixer_proj):
    rest = list(rest)
    o_ref = rest.pop()
    x = x_ref[...]
    if mixer_proj:
        y_ref, wo_ref = rest[:2]
        rest = rest[2:]
        x = x + _dot(y_ref[...], wo_ref[...])
    if final_norm:
        (gf_ref,) = rest
    h = _rms(x, g_ref[...]).astype(BF16)
    acc = x
    for c in range(D_FF // MLP_FC):
        u = _dot(h, wup_ref[:, c * MLP_FC:(c + 1) * MLP_FC])
        a = jnp.square(jnp.maximum(u, 0.0)).astype(BF16)
        acc = acc + _dot(a, wdn_ref[c * MLP_FC:(c + 1) * MLP_FC, :])
    if final_norm:
        acc = _rms(acc, gf_ref[...])
    o_ref[...] = acc


def _const_spec(shape):
    n = len(shape)
    return pl.BlockSpec(shape, lambda *_: (0,) * n, pipeline_mode=pl.Buffered(1))


def _mlp(x2, g, w_up, w_down, g_final=None, mixer_y=None, mixer_w_out=None):
    T = x2.shape[0]
    tok = pl.BlockSpec((MLP_TM, D_MODEL), lambda i: (i, 0))
    in_specs = [tok, _const_spec((1, D_MODEL)), _const_spec((D_MODEL, D_FF)), _const_spec((D_FF, D_MODEL))]
    args = [x2, g.reshape(1, D_MODEL), w_up, w_down]
    if mixer_y is not None:
        in_specs += [tok, _const_spec((D_MODEL, D_MODEL))]
        args += [mixer_y, mixer_w_out]
    if g_final is not None:
        in_specs.append(_const_spec((1, D_MODEL)))
        args.append(g_final.reshape(1, D_MODEL))
    return pl.pallas_call(
        functools.partial(_mlp_kernel, final_norm=g_final is not None, mixer_proj=mixer_y is not None),
        grid=(T // MLP_TM,),
        in_specs=in_specs,
        out_specs=tok,
        out_shape=jax.ShapeDtypeStruct((T, D_MODEL), F32),
        compiler_params=_params(1),
        name="mlp",
    )(*args)


PROJ_TM = 512


SGU_TM = 1024
SGU_PARTS = 4


def _sgu_kernel(x_ref, g_ref, win_ref, lng_ref, lnb_ref, ws_ref, bs_ref, wout_ref, o_ref, y_ref):
    part_rows = SGU_TM // SGU_PARTS
    n_chunks = part_rows // SGU_CHUNK
    parts = [dict(r0=p * part_rows) for p in range(SGU_PARTS)]
    for pt in parts:
        pt["x"] = x_ref[pt["r0"]:pt["r0"] + part_rows, :]
        pt["h"] = _rms(pt["x"], g_ref[...]).astype(BF16)
    for pt in parts:
        pt["u"] = _gelu_tanh(_dot(pt["h"], win_ref[:, :D_MODEL]))
    for pt in parts:
        v = _gelu_tanh(_dot(pt["h"], win_ref[:, D_MODEL:]))
        mu = jnp.mean(v, axis=-1, keepdims=True)
        vc = v - mu
        var = jnp.mean(vc * vc, axis=-1, keepdims=True)
        pt["v"] = (vc * lax.rsqrt(var + 1e-5) * lng_ref[...] + lnb_ref[...]).astype(BF16)
    for pt in parts:
        u, v = pt["u"], pt["v"]
        for gi in range(GB):
            cols = slice(gi * DGB, (gi + 1) * DGB)
            vg = jnp.concatenate([v[c * SGU_CHUNK:(c + 1) * SGU_CHUNK, cols] for c in range(n_chunks)], axis=1)
            vs = _dot(ws_ref[gi], vg)
            for c in range(n_chunks):
                rows = slice(c * SGU_CHUNK, (c + 1) * SGU_CHUNK)
                vsc = vs[:, c * DGB:(c + 1) * DGB] + bs_ref[gi]
                y_ref[pt["r0"] + c * SGU_CHUNK:pt["r0"] + (c + 1) * SGU_CHUNK, cols] = (u[rows, cols] * vsc).astype(BF16)
    for pt in parts:
        rows = slice(pt["r0"], pt["r0"] + part_rows)
        o_ref[rows, :] = pt["x"] + _dot(y_ref[rows, :], wout_ref[...])


def _sgu(x2, g, w_in, ln_g, ln_b, w_s, b_s, w_out):
    T = x2.shape[0]
    tok = pl.BlockSpec((SGU_TM, D_MODEL), lambda i: (i, 0))
    bs = jnp.broadcast_to(b_s[:, :, None], (GB, SGU_CHUNK, DGB)).astype(F32)
    return pl.pallas_call(
        _sgu_kernel,
        grid=(T // SGU_TM,),
        in_specs=[tok, _const_spec((1, D_MODEL)), _const_spec((D_MODEL, 2 * D_MODEL)),
                  _const_spec((1, D_MODEL)), _const_spec((1, D_MODEL)),
                  _const_spec((GB, SGU_CHUNK, SGU_CHUNK)), _const_spec((GB, SGU_CHUNK, DGB)),
                  _const_spec((D_MODEL, D_MODEL))],
        out_specs=tok,
        out_shape=jax.ShapeDtypeStruct((T, D_MODEL), F32),
        scratch_shapes=[pltpu.VMEM((SGU_TM, D_MODEL), BF16)],
        compiler_params=_params(1),
        name="sgu",
    )(x2, g.reshape(1, D_MODEL), w_in.astype(BF16), ln_g.reshape(1, D_MODEL), ln_b.reshape(1, D_MODEL),
      w_s.astype(BF16), bs, w_out.astype(BF16))


def _rglru_kernel(x_ref, g_ref, win_ref, cw_ref, cb_ref, gw_ref, gb_ref, lam_ref, o_ref,
                  hn_ref, gate_ref, xs_ref, xq_ref, a_ref, b_ref, h_ref, p_ref, ys_ref, *, seq):
    @pl.when(pl.program_id(1) == 0)
    def _():
        for rb in range(seq // RG_ROWS):
            rows = slice(rb * RG_ROWS, (rb + 1) * RG_ROWS)
            hn_ref[rows, :] = _rms(x_ref[0, rows, :], g_ref[...]).astype(BF16)

    seg = RG_SEG
    pitch = seg + SEG_PAD
    n_seg = seq // seg
    n_grp = n_seg // SUBLANES
    halo_lo, halo_hi = CONV_LEFT, CONV_W - 1 - CONV_LEFT
    grp_rows = (seg + halo_lo + halo_hi) * SUBLANES
    row = lax.broadcasted_iota(jnp.int32, (SUBLANES, HDA), 0)

    grp_tokens = SUBLANES * seg

    def project(g):
        rows = slice(g * grp_tokens, (g + 1) * grp_tokens)
        z = _dot(hn_ref[rows, :], win_ref[0])
        gate_ref[rows, :] = z[:, :HDA]
        for s in range(SUBLANES):
            q = g * SUBLANES + s
            xs_ref[q * pitch:q * pitch + seg, :] = z[s * seg:(s + 1) * seg, HDA:]

    def to_scan_order(g):
        for i in range(seg):
            r0 = g * grp_rows + (halo_lo + i) * SUBLANES
            xq_ref[r0:r0 + SUBLANES, :] = xs_ref[pl.ds(g * SUBLANES * pitch + i, SUBLANES, stride=pitch), :]

    def xq_step(g, i):
        r0 = g * grp_rows + (halo_lo + i) * SUBLANES
        return xq_ref[r0:r0 + SUBLANES, :]

    zrow = jnp.zeros((1, HDA), F32)

    def conv_halo(g):
        for i in range(seg - halo_lo, seg):
            edge = xq_step(g - 1, i)[SUBLANES - 1:SUBLANES, :] if g > 0 else zrow
            v = jnp.where(row == 0, edge, pltpu.roll(xq_step(g, i), 1, axis=0))
            r0 = g * grp_rows + (halo_lo + i - seg) * SUBLANES
            xq_ref[r0:r0 + SUBLANES, :] = v
        for i in range(halo_hi):
            edge = xq_step(g + 1, i)[0:1, :] if g < n_grp - 1 else zrow
            v = jnp.where(row == SUBLANES - 1, edge, pltpu.roll(xq_step(g, i), SUBLANES - 1, axis=0))
            r0 = g * grp_rows + (halo_lo + seg + i) * SUBLANES
            xq_ref[r0:r0 + SUBLANES, :] = v

    cw = 0.5 * cw_ref[...]
    cb = 0.5 * cb_ref[...]
    gw = gw_ref[0]
    gbias = gb_ref[0]
    hn = [(-0.5 * RG_C * LOG2E) * _softplus(-lam_ref[d:d + 1, :]) for d in range(2)]

    steps_per_blk = RG_ROWS // SUBLANES

    def gates_and_inputs(g):
        for ib in range(seg // steps_per_blk):
            src = g * grp_rows + ib * RG_ROWS
            dst = (g * seg + ib * steps_per_blk) * SUBLANES
            xh = cb
            for k in range(CONV_W):
                xh = xh + xq_ref[src + k * SUBLANES:src + k * SUBLANES + RG_ROWS, :] * cw[k:k + 1, :]
            gates = jnp.tanh(_dot(xh.astype(BF16), gw) + gbias)
            for d in range(2):
                t_r = gates[:, (2 * d) * HDA:(2 * d + 1) * HDA]
                t_i = gates[:, (2 * d + 1) * HDA:(2 * d + 2) * HDA]
                a = jnp.exp2(t_r * hn[d] + hn[d])
                y = 1.0 - a * a
                a_ref[d, dst:dst + RG_ROWS, :] = a
                b_ref[d, dst:dst + RG_ROWS, :] = (y * lax.rsqrt(jnp.maximum(y, 1e-30))) * ((t_i + 1.0) * xh)

    for g in range(min(2, n_grp)):
        project(g)
        to_scan_order(g)
    for g in range(n_grp):
        if g + 2 < n_grp:
            project(g + 2)
        conv_halo(g)
        gates_and_inputs(g)
        if g + 2 < n_grp:
            to_scan_order(g + 2)

    def scan_step(i, carry):
        out = []
        row_f = _aligned(i * SUBLANES, SUBLANES)
        row_r = _aligned((seg - 1 - i) * SUBLANES, SUBLANES)
        for g in range(n_grp):
            hf, pf, hb, pb = carry[4 * g:4 * g + 4]
            fwd = pl.ds(row_f + g * seg * SUBLANES, SUBLANES)
            rev = pl.ds(row_r + g * seg * SUBLANES, SUBLANES)
            af = a_ref[0, fwd, :]
            ab = a_ref[1, rev, :]
            hf = af * hf + b_ref[0, fwd, :]
            pf = af * pf
            hb = ab * hb + b_ref[1, rev, :]
            pb = ab * pb
            h_ref[0, fwd, :] = hf
            p_ref[0, fwd, :] = pf
            h_ref[1, rev, :] = hb
            p_ref[1, rev, :] = pb
            out += [hf, pf, hb, pb]
        return tuple(out)

    z = jnp.zeros((SUBLANES, HDA), F32)
    o = jnp.ones((SUBLANES, HDA), F32)
    ends = lax.fori_loop(0, seg, scan_step, (z, o, z, o) * n_grp, unroll=2)

    cf = [None] * n_grp
    cr = [None] * n_grp
    c_in = jnp.zeros((1, HDA), F32)
    for g in range(n_grp):
        hf, pf = ends[4 * g], ends[4 * g + 1]
        c = jnp.broadcast_to(c_in, (SUBLANES, HDA))
        for _ in range(SUBLANES - 1):
            c = jnp.where(row == 0, c_in, pltpu.roll(hf + pf * c, 1, axis=0))
        cf[g] = c
        c_in = (hf + pf * c)[SUBLANES - 1:SUBLANES, :]
    c_in = jnp.zeros((1, HDA), F32)
    for g in reversed(range(n_grp)):
        hb, pb = ends[4 * g + 2], ends[4 * g + 3]
        c = jnp.broadcast_to(c_in, (SUBLANES, HDA))
        for _ in range(SUBLANES - 1):
            c = jnp.where(row == SUBLANES - 1, c_in, pltpu.roll(hb + pb * c, SUBLANES - 1, axis=0))
        cr[g] = c
        c_in = (hb + pb * c)[0:1, :]

    for g in range(n_grp):
        cfb = jnp.tile(cf[g], (steps_per_blk, 1))
        crb = jnp.tile(cr[g], (steps_per_blk, 1))
        for ib in range(seg // steps_per_blk):
            rows = slice((g * seg + ib * steps_per_blk) * SUBLANES, (g * seg + (ib + 1) * steps_per_blk) * SUBLANES)
            ys_ref[rows, :] = ((h_ref[0, rows, :] + p_ref[0, rows, :] * cfb)
                               + (h_ref[1, rows, :] + p_ref[1, rows, :] * crb))

    for q in range(n_seg):
        g, s = divmod(q, SUBLANES)
        y = jnp.concatenate(
            [ys_ref[pl.ds((g * seg + i0) * SUBLANES + s, SUBLANES, stride=SUBLANES), :]
             for i0 in range(0, seg, SUBLANES)], axis=0)
        gate = gate_ref[q * seg:(q + 1) * seg, :]
        o_ref[0, q * seg:(q + 1) * seg, :] = (y * _gelu_tanh(gate)).astype(o_ref.dtype)


def _mixer_rglru(x2, B, S, g, w_in, conv_w, conv_b, gate_w, gate_b, lam):
    n_seg = S // RG_SEG
    xs_rows = n_seg * (RG_SEG + SEG_PAD)
    xq_rows = (n_seg // SUBLANES) * (RG_SEG + CONV_W - 1) * SUBLANES
    w2 = jnp.concatenate([w_in[:, :D_MODEL].reshape(D_MODEL, HA, HDA),
                          w_in[:, D_MODEL:].reshape(D_MODEL, HA, HDA)], axis=2).transpose(1, 0, 2).astype(BF16)
    gw = gate_w.transpose(2, 3, 0, 1, 4).reshape(HA, HDA, 4 * HDA).astype(BF16)
    gb = (0.5 * gate_b).transpose(2, 0, 1, 3).reshape(HA, 1, 4 * HDA)
    y = pl.pallas_call(
        functools.partial(_rglru_kernel, seq=S),
        grid=(B, HA),
        in_specs=[pl.BlockSpec((1, S, D_MODEL), lambda b, h: (b, 0, 0)),
                  pl.BlockSpec((1, D_MODEL), lambda b, h: (0, 0)),
                  pl.BlockSpec((1, D_MODEL, 2 * HDA), lambda b, h: (h, 0, 0)),
                  pl.BlockSpec((CONV_W, HDA), lambda b, h: (0, h)),
                  pl.BlockSpec((1, HDA), lambda b, h: (0, h)),
                  pl.BlockSpec((1, HDA, 4 * HDA), lambda b, h: (h, 0, 0)),
                  pl.BlockSpec((1, 1, 4 * HDA), lambda b, h: (h, 0, 0)),
                  pl.BlockSpec((2, HDA), lambda b, h: (0, h))],
        out_specs=pl.BlockSpec((1, S, HDA), lambda b, h: (b, 0, h)),
        out_shape=jax.ShapeDtypeStruct((B, S, D_MODEL), BF16),
        scratch_shapes=[pltpu.VMEM((S, D_MODEL), BF16), pltpu.VMEM((S, HDA), F32),
                        pltpu.VMEM((xs_rows, HDA), F32), pltpu.VMEM((xq_rows, HDA), F32),
                        pltpu.VMEM((2, S, HDA), F32), pltpu.VMEM((2, S, HDA), F32),
                        pltpu.VMEM((2, S, HDA), F32), pltpu.VMEM((2, S, HDA), F32),
                        pltpu.VMEM((S, HDA), F32)],
        compiler_params=pltpu.CompilerParams(dimension_semantics=("parallel", "arbitrary"),
                                             vmem_limit_bytes=VMEM_LIMIT),
        name="rglru",
    )(x2.reshape(B, S, D_MODEL), g.reshape(1, D_MODEL), w2, conv_w, conv_b.reshape(1, D_MODEL), gw, gb, lam)
    return y.reshape(B * S, D_MODEL)


ROW_BETA, ROW_GC, ROW_EG, ROW_EKD, ROW_ETOT = 2, 8, 16, 24, 32


def _inproj_t_kernel(x_ref, g_ref, w_ref, wt_ref, *o_refs, splits):
    h = _rms(x_ref[...], g_ref[...]).astype(BF16)
    for o_ref, (c0, c1) in zip(o_refs[:-1], splits):
        o_ref[...] = _dot(h, w_ref[:, c0:c1]).astype(o_ref.dtype)
    o_refs[-1][...] = _dot_nt(wt_ref[...], h)


def _inproj_t(x2, g, w, wt, splits, dtypes):
    T = x2.shape[0]
    n_out = w.shape[1]
    n_t = wt.shape[0]
    tok = pl.BlockSpec((PROJ_TM, D_MODEL), lambda i: (i, 0))
    return pl.pallas_call(
        functools.partial(_inproj_t_kernel, splits=splits),
        grid=(T // PROJ_TM,),
        in_specs=[tok, _const_spec((1, D_MODEL)), _const_spec((D_MODEL, n_out)), _const_spec((n_t, D_MODEL))],
        out_specs=[pl.BlockSpec((PROJ_TM, c1 - c0), lambda i: (i, 0)) for c0, c1 in splits]
        + [pl.BlockSpec((n_t, PROJ_TM), lambda i: (0, i))],
        out_shape=[jax.ShapeDtypeStruct((T, c1 - c0), dt) for (c0, c1), dt in zip(splits, dtypes)]
        + [jax.ShapeDtypeStruct((n_t, T), F32)],
        compiler_params=_params(1),
        name="inproj_t",
    )(x2, g.reshape(1, D_MODEL), w, wt)


def _gdn_kernel(q_ref, k_ref, v_ref, gate_ref, ab_ref, alog_ref, dtb_ref, cwq_ref, cwk_ref, cwv_ref, ng_ref,
                o_ref,
                xpad_ref, qn_ref, kn_ref, vv_ref, cf_ref, gcrow_ref, g_ref, bm_ref, qt_ref, qu_ref,
                *, seq):
    pad = SUBLANES
    n_blk = seq // GDN_BLOCK
    C = GDN_CHUNK
    RB = 256

    zeros = jnp.zeros((pad, DK), F32)
    xpad_ref[0:pad, :] = zeros
    xpad_ref[pad + seq:pad + seq + pad, :] = zeros

    def conv_silu(src_ref, cw_ref, dst_ref, normalize, scale):
        xpad_ref[pad:pad + seq, :] = src_ref[0]
        cw = 0.5 * cw_ref[...]
        for rb in range(seq // RB):
            r0 = rb * RB
            acc = xpad_ref[pl.ds(pad + r0 - CONV_LEFT, RB), :] * cw[0:1, :]
            for k in range(1, CONV_W):
                acc = acc + xpad_ref[pl.ds(pad + r0 + k - CONV_LEFT, RB), :] * cw[k:k + 1, :]
            y = acc * (jnp.tanh(acc) + 1.0)
            if normalize:
                y = y * (lax.rsqrt(jnp.sum(y * y, axis=-1, keepdims=True) + 1e-6) * scale)
            dst_ref[r0:r0 + RB, :] = y

    conv_silu(q_ref, cwq_ref, qn_ref, True, DK ** -0.5)
    conv_silu(k_ref, cwk_ref, kn_ref, True, 1.0)
    conv_silu(v_ref, cwv_ref, vv_ref, False, 1.0)

    ab = ab_ref[0]
    g8 = -jnp.exp(alog_ref[0][:, 0:1]) * _softplus(ab + dtb_ref[0][:, 0:1])
    beta8 = _sigmoid(ab)
    lane = lax.broadcasted_iota(jnp.int32, (SUBLANES, seq), 1) % C
    pre = g8
    suf = g8
    sh = 1
    while sh < C:
        pre = pre + jnp.where(lane >= sh, pltpu.roll(pre, sh, axis=1), 0.0)
        suf = suf + jnp.where(lane < C - sh, pltpu.roll(suf, seq - sh, axis=1), 0.0)
        sh *= 2
    tot = pre + suf - g8
    row8 = lax.broadcasted_iota(jnp.int32, (SUBLANES, seq), 0)
    gcs = jnp.where(row8 == 0, pre, suf)
    eg = jnp.exp(gcs)
    ekd = jnp.exp(tot - gcs)
    etot = jnp.exp(tot)
    fill = jnp.zeros((GDN_BLOCK - 5 * SUBLANES, GDN_BLOCK), F32)
    for b in range(n_blk):
        ls = slice(b * GDN_BLOCK, (b + 1) * GDN_BLOCK)
        table = jnp.concatenate([beta8[:, ls], gcs[:, ls], eg[:, ls], ekd[:, ls], etot[:, ls], fill], axis=0)
        cf_ref[b * GDN_BLOCK:(b + 1) * GDN_BLOCK, :] = table.T
        gcrow_ref[b] = gcs[:, ls]

    ri = lax.broadcasted_iota(jnp.int32, (C, C), 0)
    ci = lax.broadcasted_iota(jnp.int32, (C, C), 1)
    masks = (ri >= ci, ri <= ci)
    stricts = (ri > ci, ri < ci)
    eye = (ri == ci).astype(F32)
    base_blk = (ri // GDN_INV_BASE) == (ci // GDN_INV_BASE)
    sub_sizes = []
    s = GDN_INV_BASE
    while s < C:
        sub_sizes.append(s)
        s *= 2

    def take_rows(x, s, odd):
        off = s if odd else 0
        return jnp.concatenate([x[k + off:k + off + s] for k in range(0, x.shape[0], 2 * s)], axis=0)

    def put_rows(xc, s, odd):
        z = jnp.zeros((s, xc.shape[1]), xc.dtype)
        pieces = []
        for k in range(0, xc.shape[0], s):
            pieces += [z, xc[k:k + s]] if odd else [xc[k:k + s], z]
        return jnp.concatenate(pieces, axis=0)

    rh = lax.broadcasted_iota(jnp.int32, (C // 2, C), 0)
    ch_ = lax.broadcasted_iota(jnp.int32, (C // 2, C), 1)

    def couple_mask(s, odd):
        row = (rh // s) * (2 * s) + (s if odd else 0) + rh % s
        return ((row // (2 * s)) == (ch_ // (2 * s))) & ((row // s) != (ch_ // s))

    couple = tuple(tuple(couple_mask(s, odd) for s in sub_sizes) for odd in (True, False))

    def local_part(i):
        chains = []
        for bb in range(GDN_LOCAL_UNROLL):
            for d in range(2):
                b = i * GDN_LOCAL_UNROLL + bb
                if d == 1:
                    b = n_blk - 1 - b
                r0 = _aligned(b * C, C)
                kn = kn_ref[pl.ds(r0, C), :]
                qn = qn_ref[pl.ds(r0, C), :]
                kq = _dot_nt(jnp.concatenate([kn, qn], axis=0).astype(BF16), kn.astype(BF16))
                chains.append(dict(b=b, r0=r0, d=d, kk=kq[:C], qk=kq[C:]))
        yield
        for ch in chains:
            d = ch["d"]
            cf = cf_ref[pl.ds(ch["r0"], C), :]
            gcr = gcrow_ref[ch["b"]]
            beta_c = cf[:, ROW_BETA + d:ROW_BETA + d + 1]
            gc_c = cf[:, ROW_GC + d:ROW_GC + d + 1]
            decay = jnp.where(masks[d], jnp.exp(jnp.where(masks[d], gc_c - gcr[d:d + 1, :], 0.0)), 0.0)
            n_mat = jnp.where(stricts[d], -(beta_c * ch["kk"] * decay), 0.0)
            ch["qkd"] = jnp.where(masks[d], ch["qk"] * decay, 0.0).astype(BF16)
            ch["n"] = n_mat
            n0 = jnp.where(base_blk, n_mat, 0.0)
            ch["t"] = eye + n0
            ch["p16"] = n0.astype(BF16)
        for ch in chains:
            ch["p16"] = _dot(ch["p16"], ch["p16"]).astype(BF16)
        yield
        for ch in chains:
            pt = _dot(ch["p16"], jnp.concatenate([ch["p16"], ch["t"].astype(BF16)], axis=1))
            ch["p16"] = pt[:, :C].astype(BF16)
            ch["t"] = ch["t"] + pt[:, C:]
        yield
        for ch in chains:
            ch["t"] = ch["t"] + _dot(ch["p16"], ch["t"].astype(BF16))
        yield
        for lvl, s in enumerate(sub_sizes):
            for ch in chains:
                odd = ch["d"] == 0
                e16 = jnp.where(couple[ch["d"]][lvl], take_rows(ch["n"], s, odd), 0.0).astype(BF16)
                t16 = ch["t"].astype(BF16)
                ch["et"] = put_rows(_dot(e16, t16).astype(BF16), s, odd)
                ch["th"] = take_rows(t16, s, odd)
            yield
            for ch in chains:
                ch["t"] = ch["t"] + put_rows(_dot(ch["th"], ch["et"]), s, ch["d"] == 0)
            yield
        for ch in chains:
            d = ch["d"]
            rows = pl.ds(ch["r0"], C)
            cf = cf_ref[rows, :]
            kn = kn_ref[rows, :]
            beta_c = cf[:, ROW_BETA + d:ROW_BETA + d + 1]
            eg_c = cf[:, ROW_EG + d:ROW_EG + d + 1]
            rhs = jnp.concatenate([kn * (beta_c * eg_c), vv_ref[rows, :] * beta_c], axis=1).astype(BF16)
            ch["wu"] = _dot(ch["t"].astype(BF16), rhs).astype(BF16)
        yield
        for ch in chains:
            d, b = ch["d"], ch["b"]
            rows = pl.ds(ch["r0"], C)
            cf = cf_ref[rows, :]
            eg_c = cf[:, ROW_EG + d:ROW_EG + d + 1]
            ekd_c = cf[:, ROW_EKD + d:ROW_EKD + d + 1]
            kdt = (kn_ref[rows, :] * ekd_c).T.astype(BF16)
            gb = _dot(kdt, ch["wu"])
            g_ref[d, b] = (-gb[:, :DK]).astype(BF16)
            bm_ref[d, b] = gb[:, DK:]
            qwu = _dot(ch["qkd"], ch["wu"])
            qt_ref[d, rows, :] = (qn_ref[rows, :] * eg_c - qwu[:, :DK]).astype(BF16)
            qu_ref[d, rows, :] = qwu[:, DK:]

    ng = ng_ref[...]

    def state_part(i, states):
        for bb in range(GDN_LOCAL_UNROLL):
            n = i * GDN_LOCAL_UNROLL + bb
            chunks = (n, n_blk - 1 - n)
            s16 = [s.astype(BF16) for s in states]
            for d in range(2):
                r0 = chunks[d] * C
                et = cf_ref[pl.ds(r0, 1), ROW_ETOT + d:ROW_ETOT + d + 1]
                states[d] = states[d] * et + (_dot(g_ref[d, chunks[d]], s16[d]) + bm_ref[d, chunks[d]])
            for d in range(2):
                rows = pl.ds(chunks[d] * C, C)
                qu_ref[d, rows, :] = qu_ref[d, rows, :] + _dot(qt_ref[d, rows, :], s16[d])
            if 2 * n >= n_blk:
                for c in chunks:
                    rows = pl.ds(c * C, C)
                    o = qu_ref[0, rows, :] + qu_ref[1, rows, :]
                    gate = gate_ref[0, rows, :].astype(F32)
                    o_ref[0, rows, :] = (_rms(o, ng) * _silu(gate)).astype(o_ref.dtype)
            yield

    def interleave(main, filler, every):
        filler_live = True
        for k, _ in enumerate(main):
            if filler_live and k % every == every - 1:
                filler_live = next(filler, "done") != "done"
        for _ in filler:
            pass

    n_iter = n_blk // GDN_LOCAL_UNROLL
    for _ in local_part(0):
        pass

    s0 = jnp.zeros((DK, DV), F32)
    states = [s0, s0]
    for i in range(1, n_iter):
        interleave(local_part(i), state_part(i - 1, states), GDN_STATE_EVERY)
    for _ in state_part(n_iter - 1, states):
        pass


def _gdn_core(qkv3, gate3, abt3, alog8, dtb8, conv_w, norm_g):
    B, S, _ = qkv3.shape
    n_blk = S // GDN_BLOCK
    n_chunks = S // GDN_CHUNK

    def col(off):
        return pl.BlockSpec((1, S, DK), lambda b, h: (b, 0, off + h))

    def cw(off):
        return pl.BlockSpec((CONV_W, DK), lambda b, h: (0, off + h))

    par = pl.BlockSpec((1, SUBLANES, LANES), lambda b, h: (h, 0, 0))
    return pl.pallas_call(
        functools.partial(_gdn_kernel, seq=S),
        grid=(B, HC),
        in_specs=[col(0), col(HC), col(2 * HC),
                  pl.BlockSpec((1, S, DV), lambda b, h: (b, 0, h)),
                  pl.BlockSpec((1, SUBLANES, S), lambda b, h: (h, 0, b)),
                  par, par, cw(0), cw(HC), cw(2 * HC),
                  pl.BlockSpec((1, DV), lambda b, h: (0, 0))],
        out_specs=pl.BlockSpec((1, S, DV), lambda b, h: (b, 0, h)),
        out_shape=jax.ShapeDtypeStruct((B, S, DC), BF16),
        scratch_shapes=[pltpu.VMEM((S + 2 * SUBLANES, DK), F32),
                        pltpu.VMEM((S, DK), F32), pltpu.VMEM((S, DK), F32), pltpu.VMEM((S, DV), F32),
                        pltpu.VMEM((S, LANES), F32), pltpu.VMEM((n_blk, SUBLANES, GDN_BLOCK), F32),
                        pltpu.VMEM((2, n_chunks, DK, DK), BF16), pltpu.VMEM((2, n_chunks, DK, DV), F32),
                        pltpu.VMEM((2, S, DK), BF16), pltpu.VMEM((2, S, DV), F32)],
        compiler_params=_params(2),
        name="gdn",
    )(qkv3, qkv3, qkv3, gate3, abt3, alog8, dtb8, conv_w, conv_w, conv_w, norm_g.reshape(1, DV))


def _mixer_gdn(x2, B, S, g, w_in, conv_w, a_log, dt_bias, norm_g):
    w_ab = w_in[:, 4 * DC:].reshape(D_MODEL, 2, 2, HC).transpose(3, 1, 2, 0).reshape(HC, 4, D_MODEL)
    w_ab = jnp.concatenate([w_ab, jnp.zeros_like(w_ab)], axis=1).reshape(HC * SUBLANES, D_MODEL)
    qkv, gate, abt = _inproj_t(x2, g, w_in[:, :4 * DC].astype(BF16), w_ab.astype(BF16),
                               ((0, 3 * DC), (3 * DC, 4 * DC)), (F32, BF16))

    def per_head(p):
        p8 = jnp.concatenate([p.T.astype(F32), jnp.zeros((HC, SUBLANES - 2), F32)], axis=1)
        return jnp.broadcast_to(p8[:, :, None], (HC, SUBLANES, LANES))

    y = _gdn_core(qkv.reshape(B, S, 3 * DC), gate.reshape(B, S, DC), abt.reshape(HC, SUBLANES, B * S),
                  per_head(a_log), per_head(dt_bias), conv_w, norm_g)
    return y.reshape(B * S, DC)


def kernel(x, norm_mix_g, norm_mlp_g, mlp_w_up, mlp_w_down, norm_final_g, a_w_in, a_conv_w, a_conv_b, a_gate_w, a_gate_b, a_lambda, a_w_out, b_w_in, b_ln_g, b_ln_b, b_w_s, b_b_s, b_w_out, c_w_in, c_conv_w, c_a_log, c_dt_bias, c_norm_g, c_w_out):
    B, S, _ = x.shape
    depth = norm_mix_g.shape[0]
    x2 = x.reshape(B * S, D_MODEL)
    for i in range(depth):
        kind, j = i % N_MIXERS, i // N_MIXERS
        g = norm_mix_g[i]
        y, w_out = None, None
        if kind == 0:
            y = _mixer_rglru(x2, B, S, g, a_w_in[j], a_conv_w[j], a_conv_b[j], a_gate_w[j], a_gate_b[j], a_lambda[j])
            w_out = a_w_out[j].astype(BF16)
        elif kind == 1:
            x2 = _sgu(x2, g, b_w_in[j], b_ln_g[j], b_ln_b[j], b_w_s[j], b_b_s[j], b_w_out[j])
        else:
            y = _mixer_gdn(x2, B, S, g, c_w_in[j], c_conv_w[j], c_a_log[j], c_dt_bias[j], c_norm_g[j])
            w_out = c_w_out[j].astype(BF16)
        x2 = _mlp(x2, norm_mlp_g[i], mlp_w_up[i].astype(BF16), mlp_w_down[i].astype(BF16),
                  norm_final_g if i == depth - 1 else None, y, w_out)
    return x2.reshape(B, S, D_MODEL)
```

```python
import functools

import jax
import jax.numpy as jnp
from jax import lax
from jax.experimental import pallas as pl
from jax.experimental.pallas import tpu as pltpu

F32 = jnp.float32
BF16 = jnp.bfloat16

D_MODEL = 1024
D_FF = 4 * D_MODEL
N_MIXERS = 3
LANES = 128
SUBLANES = 8
VMEM_LIMIT = 56 * 1024 * 1024

HA = 8
HDA = D_MODEL // HA
CONV_W = 4
CONV_LEFT = CONV_W // 2
RG_C = 8.0
LOG2E = 1.4426950408889634
RG_SEG = 64
SEG_PAD = 8
RG_ROWS = 256
SGU_CHUNK = 128
GB = 8
DGB = D_MODEL // GB
HC = 8
DK = 128
DV = 128
DC = HC * DV
GDN_CHUNK = LANES
GDN_BLOCK = GDN_CHUNK
GDN_INV_BASE = 8
GDN_GROUPS = (11, 5)
GDN_STATE_EVERY = 1


def _params(n_parallel):
    return pltpu.CompilerParams(dimension_semantics=("parallel",) * n_parallel,
                                vmem_limit_bytes=VMEM_LIMIT)


def _rms(xf, g, eps=1e-6):
    return xf * lax.rsqrt(jnp.mean(xf * xf, axis=-1, keepdims=True) + eps) * g


def _sigmoid(x):
    return 0.5 * jnp.tanh(0.5 * x) + 0.5


def _silu(x):
    return x * _sigmoid(x)


def _gelu_tanh(x):
    c = 0.7978845608028654
    return 0.5 * x * (1.0 + jnp.tanh(c * (x + 0.044715 * (x * x * x))))


def _softplus(x):
    return jnp.maximum(x, 0.0) + jnp.log1p(jnp.exp(-jnp.abs(x)))


def _dot(a, b):
    return jnp.dot(a, b, preferred_element_type=F32)


def _aligned(offset, multiple):
    return offset if isinstance(offset, int) else pl.multiple_of(offset, multiple)


def _dot_nt(a, b):
    return lax.dot_general(a, b, (((1,), (1,)), ((), ())), preferred_element_type=F32)


MLP_TM = 512
MLP_FC = 1024


def _mlp_kernel(x_ref, g_ref, wup_ref, wdn_ref, *rest, final_norm, mixer_proj):
    rest = list(rest)
    o_ref = rest.pop()
    x = x_ref[...]
    if mixer_proj:
        y_ref, wo_ref = rest[:2]
        rest = rest[2:]
        x = x + _dot(y_ref[...], wo_ref[...])
    if final_norm:
        (gf_ref,) = rest
    h = _rms(x, g_ref[...]).astype(BF16)
    acc = x
    for c in range(D_FF // MLP_FC):
        u = _dot(h, wup_ref[:, c * MLP_FC:(c + 1) * MLP_FC])
        a = jnp.square(jnp.maximum(u, 0.0)).astype(BF16)
        acc = acc + _dot(a, wdn_ref[c * MLP_FC:(c + 1) * MLP_FC, :])
    if final_norm:
        acc = _rms(acc, gf_ref[...])
    o_ref[...] = acc


def _const_spec(shape):
    n = len(shape)
    return pl.BlockSpec(shape, lambda *_: (0,) * n, pipeline_mode=pl.Buffered(1))


def _mlp(x2, g, w_up, w_down, g_final=None, mixer_y=None, mixer_w_out=None):
    T = x2.shape[0]
    tok = pl.BlockSpec((MLP_TM, D_MODEL), lambda i: (i, 0))
    in_specs = [tok, _const_spec((1, D_MODEL)), _const_spec((D_MODEL, D_FF)), _const_spec((D_FF, D_MODEL))]
    args = [x2, g.reshape(1, D_MODEL), w_up, w_down]
    if mixer_y is not None:
        in_specs += [tok, _const_spec((D_MODEL, D_MODEL))]
        args += [mixer_y, mixer_w_out]
    if g_final is not None:
        in_specs.append(_const_spec((1, D_MODEL)))
        args.append(g_final.reshape(1, D_MODEL))
    return pl.pallas_call(
        functools.partial(_mlp_kernel, final_norm=g_final is not None, mixer_proj=mixer_y is not None),
        grid=(T // MLP_TM,),
        in_specs=in_specs,
        out_specs=tok,
        out_shape=jax.ShapeDtypeStruct((T, D_MODEL), F32),
        compiler_params=_params(1),
        name="mlp",
    )(*args)


PROJ_TM = 512


SGU_TM = 1024
SGU_PARTS = 4


def _sgu_kernel(x_ref, g_ref, win_ref, lng_ref, lnb_ref, ws_ref, bs_ref, wout_ref, o_ref, y_ref):
    part_rows = SGU_TM // SGU_PARTS
    n_chunks = part_rows // SGU_CHUNK
    parts = [dict(r0=p * part_rows) for p in range(SGU_PARTS)]
    for pt in parts:
        pt["x"] = x_ref[pt["r0"]:pt["r0"] + part_rows, :]
        pt["h"] = _rms(pt["x"], g_ref[...]).astype(BF16)
    for pt in parts:
        pt["u"] = _gelu_tanh(_dot(pt["h"], win_ref[:, :D_MODEL]))
    for pt in parts:
        v = _gelu_tanh(_dot(pt["h"], win_ref[:, D_MODEL:]))
        mu = jnp.mean(v, axis=-1, keepdims=True)
        vc = v - mu
        var = jnp.mean(vc * vc, axis=-1, keepdims=True)
        pt["v"] = (vc * lax.rsqrt(var + 1e-5) * lng_ref[...] + lnb_ref[...]).astype(BF16)
    for pt in parts:
        u, v = pt["u"], pt["v"]
        for gi in range(GB):
            cols = slice(gi * DGB, (gi + 1) * DGB)
            vg = jnp.concatenate([v[c * SGU_CHUNK:(c + 1) * SGU_CHUNK, cols] for c in range(n_chunks)], axis=1)
            vs = _dot(ws_ref[gi], vg)
            for c in range(n_chunks):
                rows = slice(c * SGU_CHUNK, (c + 1) * SGU_CHUNK)
                vsc = vs[:, c * DGB:(c + 1) * DGB] + bs_ref[gi]
                y_ref[pt["r0"] + c * SGU_CHUNK:pt["r0"] + (c + 1) * SGU_CHUNK, cols] = (u[rows, cols] * vsc).astype(BF16)
    for pt in parts:
        rows = slice(pt["r0"], pt["r0"] + part_rows)
        o_ref[rows, :] = pt["x"] + _dot(y_ref[rows, :], wout_ref[...])


def _sgu(x2, g, w_in, ln_g, ln_b, w_s, b_s, w_out):
    T = x2.shape[0]
    tok = pl.BlockSpec((SGU_TM, D_MODEL), lambda i: (i, 0))
    bs = jnp.broadcast_to(b_s[:, :, None], (GB, SGU_CHUNK, DGB)).astype(F32)
    return pl.pallas_call(
        _sgu_kernel,
        grid=(T // SGU_TM,),
        in_specs=[tok, _const_spec((1, D_MODEL)), _const_spec((D_MODEL, 2 * D_MODEL)),
                  _const_spec((1, D_MODEL)), _const_spec((1, D_MODEL)),
                  _const_spec((GB, SGU_CHUNK, SGU_CHUNK)), _const_spec((GB, SGU_CHUNK, DGB)),
                  _const_spec((D_MODEL, D_MODEL))],
        out_specs=tok,
        out_shape=jax.ShapeDtypeStruct((T, D_MODEL), F32),
        scratch_shapes=[pltpu.VMEM((SGU_TM, D_MODEL), BF16)],
        compiler_params=_params(1),
        name="sgu",
    )(x2, g.reshape(1, D_MODEL), w_in.astype(BF16), ln_g.reshape(1, D_MODEL), ln_b.reshape(1, D_MODEL),
      w_s.astype(BF16), bs, w_out.astype(BF16))


def _rglru_kernel(x_ref, g_ref, win_ref, cw_ref, cb_ref, gw_ref, gb_ref, lam_ref, o_ref,
                  hn_ref, gate_ref, xs_ref, xq_ref, a_ref, b_ref, h_ref, p_ref, ys_ref, *, seq):
    @pl.when(pl.program_id(1) == 0)
    def _():
        for rb in range(seq // RG_ROWS):
            rows = slice(rb * RG_ROWS, (rb + 1) * RG_ROWS)
            hn_ref[rows, :] = _rms(x_ref[0, rows, :], g_ref[...]).astype(BF16)

    seg = RG_SEG
    pitch = seg + SEG_PAD
    n_seg = seq // seg
    n_grp = n_seg // SUBLANES
    halo_lo, halo_hi = CONV_LEFT, CONV_W - 1 - CONV_LEFT
    grp_rows = (seg + halo_lo + halo_hi) * SUBLANES
    row = lax.broadcasted_iota(jnp.int32, (SUBLANES, HDA), 0)

    grp_tokens = SUBLANES * seg

    def project(g):
        rows = slice(g * grp_tokens, (g + 1) * grp_tokens)
        z = _dot(hn_ref[rows, :], win_ref[0])
        gate_ref[rows, :] = z[:, :HDA]
        for s in range(SUBLANES):
            q = g * SUBLANES + s
            xs_ref[q * pitch:q * pitch + seg, :] = z[s * seg:(s + 1) * seg, HDA:]

    def to_scan_order(g):
        for i in range(seg):
            r0 = g * grp_rows + (halo_lo + i) * SUBLANES
            xq_ref[r0:r0 + SUBLANES, :] = xs_ref[pl.ds(g * SUBLANES * pitch + i, SUBLANES, stride=pitch), :]

    def xq_step(g, i):
        r0 = g * grp_rows + (halo_lo + i) * SUBLANES
        return xq_ref[r0:r0 + SUBLANES, :]

    zrow = jnp.zeros((1, HDA), F32)

    def conv_halo(g):
        for i in range(seg - halo_lo, seg):
            edge = xq_step(g - 1, i)[SUBLANES - 1:SUBLANES, :] if g > 0 else zrow
            v = jnp.where(row == 0, edge, pltpu.roll(xq_step(g, i), 1, axis=0))
            r0 = g * grp_rows + (halo_lo + i - seg) * SUBLANES
            xq_ref[r0:r0 + SUBLANES, :] = v
        for i in range(halo_hi):
            edge = xq_step(g + 1, i)[0:1, :] if g < n_grp - 1 else zrow
            v = jnp.where(row == SUBLANES - 1, edge, pltpu.roll(xq_step(g, i), SUBLANES - 1, axis=0))
            r0 = g * grp_rows + (halo_lo + seg + i) * SUBLANES
            xq_ref[r0:r0 + SUBLANES, :] = v

    cw = 0.5 * cw_ref[...]
    cb = 0.5 * cb_ref[...]
    gw = gw_ref[0]
    gbias = gb_ref[0]
    hn = [(-0.5 * RG_C * LOG2E) * _softplus(-lam_ref[d:d + 1, :]) for d in range(2)]

    steps_per_blk = RG_ROWS // SUBLANES

    def gates_and_inputs(g):
        for ib in range(seg // steps_per_blk):
            src = g * grp_rows + ib * RG_ROWS
            dst = (g * seg + ib * steps_per_blk) * SUBLANES
            xh = cb
            for k in range(CONV_W):
                xh = xh + xq_ref[src + k * SUBLANES:src + k * SUBLANES + RG_ROWS, :] * cw[k:k + 1, :]
            gates = jnp.tanh(_dot(xh.astype(BF16), gw) + gbias)
            for d in range(2):
                t_r = gates[:, (2 * d) * HDA:(2 * d + 1) * HDA]
                t_i = gates[:, (2 * d + 1) * HDA:(2 * d + 2) * HDA]
                a = jnp.exp2(t_r * hn[d] + hn[d])
                y = 1.0 - a * a
                a_ref[d, dst:dst + RG_ROWS, :] = a
                b_ref[d, dst:dst + RG_ROWS, :] = (y * lax.rsqrt(jnp.maximum(y, 1e-30))) * ((t_i + 1.0) * xh)

    for g in range(min(2, n_grp)):
        project(g)
        to_scan_order(g)
    for g in range(n_grp):
        if g + 2 < n_grp:
            project(g + 2)
        conv_halo(g)
        gates_and_inputs(g)
        if g + 2 < n_grp:
            to_scan_order(g + 2)

    def scan_step(i, carry):
        out = []
        row_f = _aligned(i * SUBLANES, SUBLANES)
        row_r = _aligned((seg - 1 - i) * SUBLANES, SUBLANES)
        for g in range(n_grp):
            hf, pf, hb, pb = carry[4 * g:4 * g + 4]
            fwd = pl.ds(row_f + g * seg * SUBLANES, SUBLANES)
            rev = pl.ds(row_r + g * seg * SUBLANES, SUBLANES)
            af = a_ref[0, fwd, :]
            ab = a_ref[1, rev, :]
            hf = af * hf + b_ref[0, fwd, :]
            pf = af * pf
            hb = ab * hb + b_ref[1, rev, :]
            pb = ab * pb
            h_ref[0, fwd, :] = hf
            p_ref[0, fwd, :] = pf
            h_ref[1, rev, :] = hb
            p_ref[1, rev, :] = pb
            out += [hf, pf, hb, pb]
        return tuple(out)

    z = jnp.zeros((SUBLANES, HDA), F32)
    o = jnp.ones((SUBLANES, HDA), F32)
    ends = lax.fori_loop(0, seg, scan_step, (z, o, z, o) * n_grp, unroll=2)

    cf = [None] * n_grp
    cr = [None] * n_grp
    c_in = jnp.zeros((1, HDA), F32)
    for g in range(n_grp):
        hf, pf = ends[4 * g], ends[4 * g + 1]
        c = jnp.broadcast_to(c_in, (SUBLANES, HDA))
        for _ in range(SUBLANES - 1):
            c = jnp.where(row == 0, c_in, pltpu.roll(hf + pf * c, 1, axis=0))
        cf[g] = c
        c_in = (hf + pf * c)[SUBLANES - 1:SUBLANES, :]
    c_in = jnp.zeros((1, HDA), F32)
    for g in reversed(range(n_grp)):
        hb, pb = ends[4 * g + 2], ends[4 * g + 3]
        c = jnp.broadcast_to(c_in, (SUBLANES, HDA))
        for _ in range(SUBLANES - 1):
            c = jnp.where(row == SUBLANES - 1, c_in, pltpu.roll(hb + pb * c, SUBLANES - 1, axis=0))
        cr[g] = c
        c_in = (hb + pb * c)[0:1, :]

    for g in range(n_grp):
        cfb = jnp.tile(cf[g], (steps_per_blk, 1))
        crb = jnp.tile(cr[g], (steps_per_blk, 1))
        for ib in range(seg // steps_per_blk):
            rows = slice((g * seg + ib * steps_per_blk) * SUBLANES, (g * seg + (ib + 1) * steps_per_blk) * SUBLANES)
            ys_ref[rows, :] = ((h_ref[0, rows, :] + p_ref[0, rows, :] * cfb)
                               + (h_ref[1, rows, :] + p_ref[1, rows, :] * crb))

    for q in range(n_seg):
        g, s = divmod(q, SUBLANES)
        y = jnp.concatenate(
            [ys_ref[pl.ds((g * seg + i0) * SUBLANES + s, SUBLANES, stride=SUBLANES), :]
             for i0 in range(0, seg, SUBLANES)], axis=0)
        gate = gate_ref[q * seg:(q + 1) * seg, :]
        o_ref[0, q * seg:(q + 1) * seg, :] = (y * _gelu_tanh(gate)).astype(o_ref.dtype)


def _mixer_rglru(x2, B, S, g, w_in, conv_w, conv_b, gate_w, gate_b, lam):
    n_seg = S // RG_SEG
    xs_rows = n_seg * (RG_SEG + SEG_PAD)
    xq_rows = (n_seg // SUBLANES) * (RG_SEG + CONV_W - 1) * SUBLANES
    w2 = jnp.concatenate([w_in[:, :D_MODEL].reshape(D_MODEL, HA, HDA),
                          w_in[:, D_MODEL:].reshape(D_MODEL, HA, HDA)], axis=2).transpose(1, 0, 2).astype(BF16)
    gw = gate_w.transpose(2, 3, 0, 1, 4).reshape(HA, HDA, 4 * HDA).astype(BF16)
    gb = (0.5 * gate_b).transpose(2, 0, 1, 3).reshape(HA, 1, 4 * HDA)
    y = pl.pallas_call(
        functools.partial(_rglru_kernel, seq=S),
        grid=(B, HA),
        in_specs=[pl.BlockSpec((1, S, D_MODEL), lambda b, h: (b, 0, 0)),
                  pl.BlockSpec((1, D_MODEL), lambda b, h: (0, 0)),
                  pl.BlockSpec((1, D_MODEL, 2 * HDA), lambda b, h: (h, 0, 0)),
                  pl.BlockSpec((CONV_W, HDA), lambda b, h: (0, h)),
                  pl.BlockSpec((1, HDA), lambda b, h: (0, h)),
                  pl.BlockSpec((1, HDA, 4 * HDA), lambda b, h: (h, 0, 0)),
                  pl.BlockSpec((1, 1, 4 * HDA), lambda b, h: (h, 0, 0)),
                  pl.BlockSpec((2, HDA), lambda b, h: (0, h))],
        out_specs=pl.BlockSpec((1, S, HDA), lambda b, h: (b, 0, h)),
        out_shape=jax.ShapeDtypeStruct((B, S, D_MODEL), BF16),
        scratch_shapes=[pltpu.VMEM((S, D_MODEL), BF16), pltpu.VMEM((S, HDA), F32),
                        pltpu.VMEM((xs_rows, HDA), F32), pltpu.VMEM((xq_rows, HDA), F32),
                        pltpu.VMEM((2, S, HDA), F32), pltpu.VMEM((2, S, HDA), F32),
                        pltpu.VMEM((2, S, HDA), F32), pltpu.VMEM((2, S, HDA), F32),
                        pltpu.VMEM((S, HDA), F32)],
        compiler_params=pltpu.CompilerParams(dimension_semantics=("parallel", "arbitrary"),
                                             vmem_limit_bytes=VMEM_LIMIT),
        name="rglru",
    )(x2.reshape(B, S, D_MODEL), g.reshape(1, D_MODEL), w2, conv_w, conv_b.reshape(1, D_MODEL), gw, gb, lam)
    return y.reshape(B * S, D_MODEL)


ROW_BETA, ROW_GC, ROW_EG, ROW_EKD, ROW_ETOT = 2, 8, 16, 24, 32


def _inproj_t_kernel(x_ref, g_ref, w_ref, wt_ref, *o_refs, splits):
    h = _rms(x_ref[...], g_ref[...]).astype(BF16)
    for o_ref, (c0, c1) in zip(o_refs[:-1], splits):
        o_ref[...] = _dot(h, w_ref[:, c0:c1]).astype(o_ref.dtype)
    o_refs[-1][...] = _dot_nt(wt_ref[...], h)


def _inproj_t(x2, g, w, wt, splits, dtypes):
    T = x2.shape[0]
    n_out = w.shape[1]
    n_t = wt.shape[0]
    tok = pl.BlockSpec((PROJ_TM, D_MODEL), lambda i: (i, 0))
    return pl.pallas_call(
        functools.partial(_inproj_t_kernel, splits=splits),
        grid=(T // PROJ_TM,),
        in_specs=[tok, _const_spec((1, D_MODEL)), _const_spec((D_MODEL, n_out)), _const_spec((n_t, D_MODEL))],
        out_specs=[pl.BlockSpec((PROJ_TM, c1 - c0), lambda i: (i, 0)) for c0, c1 in splits]
        + [pl.BlockSpec((n_t, PROJ_TM), lambda i: (0, i))],
        out_shape=[jax.ShapeDtypeStruct((T, c1 - c0), dt) for (c0, c1), dt in zip(splits, dtypes)]
        + [jax.ShapeDtypeStruct((n_t, T), F32)],
        compiler_params=_params(1),
        name="inproj_t",
    )(x2, g.reshape(1, D_MODEL), w, wt)


def _gdn_kernel(q_ref, k_ref, v_ref, gate_ref, ab_ref, alog_ref, dtb_ref, cwq_ref, cwk_ref, cwv_ref, ng_ref,
                o_ref,
                xpad_ref, qn_ref, kn_ref, vv_ref, cf_ref, gcrow_ref, g_ref, bm_ref, qt_ref, qu_ref,
                *, seq):
    pad = SUBLANES
    n_blk = seq // GDN_BLOCK
    C = GDN_CHUNK
    RB = 256

    zeros = jnp.zeros((pad, DK), F32)
    xpad_ref[0:pad, :] = zeros
    xpad_ref[pad + seq:pad + seq + pad, :] = zeros

    def conv_silu(src_ref, cw_ref, dst_ref, normalize, scale):
        xpad_ref[pad:pad + seq, :] = src_ref[0]
        cw = 0.5 * cw_ref[...]
        for rb in range(seq // RB):
            r0 = rb * RB
            acc = xpad_ref[pl.ds(pad + r0 - CONV_LEFT, RB), :] * cw[0:1, :]
            for k in range(1, CONV_W):
                acc = acc + xpad_ref[pl.ds(pad + r0 + k - CONV_LEFT, RB), :] * cw[k:k + 1, :]
            y = acc * (jnp.tanh(acc) + 1.0)
            if normalize:
                y = y * (lax.rsqrt(jnp.sum(y * y, axis=-1, keepdims=True) + 1e-6) * scale)
            dst_ref[r0:r0 + RB, :] = y

    conv_silu(q_ref, cwq_ref, qn_ref, True, DK ** -0.5)
    conv_silu(k_ref, cwk_ref, kn_ref, True, 1.0)
    conv_silu(v_ref, cwv_ref, vv_ref, False, 1.0)

    ab = ab_ref[0]
    g8 = -jnp.exp(alog_ref[0][:, 0:1]) * _softplus(ab + dtb_ref[0][:, 0:1])
    beta8 = _sigmoid(ab)
    lane = lax.broadcasted_iota(jnp.int32, (SUBLANES, seq), 1) % C
    pre = g8
    suf = g8
    sh = 1
    while sh < C:
        pre = pre + jnp.where(lane >= sh, pltpu.roll(pre, sh, axis=1), 0.0)
        suf = suf + jnp.where(lane < C - sh, pltpu.roll(suf, seq - sh, axis=1), 0.0)
        sh *= 2
    tot = pre + suf - g8
    row8 = lax.broadcasted_iota(jnp.int32, (SUBLANES, seq), 0)
    gcs = jnp.where(row8 == 0, pre, suf)
    eg = jnp.exp(gcs)
    ekd = jnp.exp(tot - gcs)
    etot = jnp.exp(tot)
    fill = jnp.zeros((GDN_BLOCK - 5 * SUBLANES, GDN_BLOCK), F32)
    for b in range(n_blk):
        ls = slice(b * GDN_BLOCK, (b + 1) * GDN_BLOCK)
        table = jnp.concatenate([beta8[:, ls], gcs[:, ls], eg[:, ls], ekd[:, ls], etot[:, ls], fill], axis=0)
        cf_ref[b * GDN_BLOCK:(b + 1) * GDN_BLOCK, :] = table.T
        gcrow_ref[b] = gcs[:, ls]

    ri = lax.broadcasted_iota(jnp.int32, (C, C), 0)
    ci = lax.broadcasted_iota(jnp.int32, (C, C), 1)
    masks = (ri >= ci, ri <= ci)
    stricts = (ri > ci, ri < ci)
    eye = (ri == ci).astype(F32)
    base_blk = (ri // GDN_INV_BASE) == (ci // GDN_INV_BASE)
    sub_sizes = []
    s = GDN_INV_BASE
    while s < C:
        sub_sizes.append(s)
        s *= 2

    def take_rows(x, s, odd):
        off = s if odd else 0
        return jnp.concatenate([x[k + off:k + off + s] for k in range(0, x.shape[0], 2 * s)], axis=0)

    def put_rows(xc, s, odd):
        z = jnp.zeros((s, xc.shape[1]), xc.dtype)
        pieces = []
        for k in range(0, xc.shape[0], s):
            pieces += [z, xc[k:k + s]] if odd else [xc[k:k + s], z]
        return jnp.concatenate(pieces, axis=0)

    rh = lax.broadcasted_iota(jnp.int32, (C // 2, C), 0)
    ch_ = lax.broadcasted_iota(jnp.int32, (C // 2, C), 1)

    def couple_mask(s, odd):
        row = (rh // s) * (2 * s) + (s if odd else 0) + rh % s
        return ((row // (2 * s)) == (ch_ // (2 * s))) & ((row // s) != (ch_ // s))

    couple = tuple(tuple(couple_mask(s, odd) for s in sub_sizes) for odd in (True, False))

    bounds = [0]
    for size in GDN_GROUPS:
        bounds.append(bounds[-1] + size)
    assert bounds[-1] == n_blk

    def local_part(i):
        chains = []
        for pos in range(bounds[i], bounds[i + 1]):
            for d in range(2):
                b = pos
                if d == 1:
                    b = n_blk - 1 - b
                r0 = _aligned(b * C, C)
                kn = kn_ref[pl.ds(r0, C), :]
                qn = qn_ref[pl.ds(r0, C), :]
                kq = _dot_nt(jnp.concatenate([kn, qn], axis=0).astype(BF16), kn.astype(BF16))
                chains.append(dict(b=b, r0=r0, d=d, kk=kq[:C], qk=kq[C:]))
        yield
        for ch in chains:
            d = ch["d"]
            cf = cf_ref[pl.ds(ch["r0"], C), :]
            gcr = gcrow_ref[ch["b"]]
            beta_c = cf[:, ROW_BETA + d:ROW_BETA + d + 1]
            gc_c = cf[:, ROW_GC + d:ROW_GC + d + 1]
            decay = jnp.where(masks[d], jnp.exp(jnp.where(masks[d], gc_c - gcr[d:d + 1, :], 0.0)), 0.0)
            n_mat = jnp.where(stricts[d], -(beta_c * ch["kk"] * decay), 0.0)
            ch["qkd"] = jnp.where(masks[d], ch["qk"] * decay, 0.0).astype(BF16)
            ch["n"] = n_mat
            n0 = jnp.where(base_blk, n_mat, 0.0)
            ch["t"] = eye + n0
            ch["p16"] = n0.astype(BF16)
        for ch in chains:
            ch["p16"] = _dot(ch["p16"], ch["p16"]).astype(BF16)
        yield
        for ch in chains:
            pt = _dot(ch["p16"], jnp.concatenate([ch["p16"], ch["t"].astype(BF16)], axis=1))
            ch["p16"] = pt[:, :C].astype(BF16)
            ch["t"] = ch["t"] + pt[:, C:]
        yield
        for ch in chains:
            ch["t"] = ch["t"] + _dot(ch["p16"], ch["t"].astype(BF16))
        yield
        for lvl, s in enumerate(sub_sizes):
            for ch in chains:
                odd = ch["d"] == 0
                e16 = jnp.where(couple[ch["d"]][lvl], take_rows(ch["n"], s, odd), 0.0).astype(BF16)
                t16 = ch["t"].astype(BF16)
                ch["et"] = put_rows(_dot(e16, t16).astype(BF16), s, odd)
                ch["th"] = take_rows(t16, s, odd)
            yield
            for ch in chains:
                ch["t"] = ch["t"] + put_rows(_dot(ch["th"], ch["et"]), s, ch["d"] == 0)
            yield
        for ch in chains:
            d = ch["d"]
            rows = pl.ds(ch["r0"], C)
            cf = cf_ref[rows, :]
            kn = kn_ref[rows, :]
            beta_c = cf[:, ROW_BETA + d:ROW_BETA + d + 1]
            eg_c = cf[:, ROW_EG + d:ROW_EG + d + 1]
            rhs = jnp.concatenate([kn * (beta_c * eg_c), vv_ref[rows, :] * beta_c], axis=1).astype(BF16)
            ch["wu"] = _dot(ch["t"].astype(BF16), rhs).astype(BF16)
        yield
        for ch in chains:
            d, b = ch["d"], ch["b"]
            rows = pl.ds(ch["r0"], C)
            cf = cf_ref[rows, :]
            eg_c = cf[:, ROW_EG + d:ROW_EG + d + 1]
            ekd_c = cf[:, ROW_EKD + d:ROW_EKD + d + 1]
            kdt = (kn_ref[rows, :] * ekd_c).T.astype(BF16)
            gb = _dot(kdt, ch["wu"])
            g_ref[d, b] = (-gb[:, :DK]).astype(BF16)
            bm_ref[d, b] = gb[:, DK:]
            qwu = _dot(ch["qkd"], ch["wu"])
            qt_ref[d, rows, :] = (qn_ref[rows, :] * eg_c - qwu[:, :DK]).astype(BF16)
            qu_ref[d, rows, :] = qwu[:, DK:]

    ng = ng_ref[...]

    def state_part(i, states):
        for n in range(bounds[i], bounds[i + 1]):
            chunks = (n, n_blk - 1 - n)
            s16 = [s.astype(BF16) for s in states]
            for d in range(2):
                r0 = chunks[d] * C
                et = cf_ref[pl.ds(r0, 1), ROW_ETOT + d:ROW_ETOT + d + 1]
                states[d] = states[d] * et + (_dot(g_ref[d, chunks[d]], s16[d]) + bm_ref[d, chunks[d]])
            for d in range(2):
                rows = pl.ds(chunks[d] * C, C)
                qu_ref[d, rows, :] = qu_ref[d, rows, :] + _dot(qt_ref[d, rows, :], s16[d])
            if 2 * n >= n_blk:
                for c in chunks:
                    rows = pl.ds(c * C, C)
                    o = qu_ref[0, rows, :] + qu_ref[1, rows, :]
                    gate = gate_ref[0, rows, :].astype(F32)
                    o_ref[0, rows, :] = (_rms(o, ng) * _silu(gate)).astype(o_ref.dtype)
            yield

    def interleave(main, filler, every):
        filler_live = True
        for k, _ in enumerate(main):
            if filler_live and k % every == every - 1:
                filler_live = next(filler, "done") != "done"
        for _ in filler:
            pass

    n_iter = len(GDN_GROUPS)
    for _ in local_part(0):
        pass

    s0 = jnp.zeros((DK, DV), F32)
    states = [s0, s0]
    for i in range(1, n_iter):
        interleave(local_part(i), state_part(i - 1, states), GDN_STATE_EVERY)
    for _ in state_part(n_iter - 1, states):
        pass


def _gdn_core(qkv3, gate3, abt3, alog8, dtb8, conv_w, norm_g):
    B, S, _ = qkv3.shape
    n_blk = S // GDN_BLOCK
    n_chunks = S // GDN_CHUNK

    def col(off):
        return pl.BlockSpec((1, S, DK), lambda b, h: (b, 0, off + h))

    def cw(off):
        return pl.BlockSpec((CONV_W, DK), lambda b, h: (0, off + h))

    par = pl.BlockSpec((1, SUBLANES, LANES), lambda b, h: (h, 0, 0))
    return pl.pallas_call(
        functools.partial(_gdn_kernel, seq=S),
        grid=(B, HC),
        in_specs=[col(0), col(HC), col(2 * HC),
                  pl.BlockSpec((1, S, DV), lambda b, h: (b, 0, h)),
                  pl.BlockSpec((1, SUBLANES, S), lambda b, h: (h, 0, b)),
                  par, par, cw(0), cw(HC), cw(2 * HC),
                  pl.BlockSpec((1, DV), lambda b, h: (0, 0))],
        out_specs=pl.BlockSpec((1, S, DV), lambda b, h: (b, 0, h)),
        out_shape=jax.ShapeDtypeStruct((B, S, DC), BF16),
        scratch_shapes=[pltpu.VMEM((S + 2 * SUBLANES, DK), F32),
                        pltpu.VMEM((S, DK), F32), pltpu.VMEM((S, DK), F32), pltpu.VMEM((S, DV), F32),
                        pltpu.VMEM((S, LANES), F32), pltpu.VMEM((n_blk, SUBLANES, GDN_BLOCK), F32),
                        pltpu.VMEM((2, n_chunks, DK, DK), BF16), pltpu.VMEM((2, n_chunks, DK, DV), F32),
                        pltpu.VMEM((2, S, DK), BF16), pltpu.VMEM((2, S, DV), F32)],
        compiler_params=_params(2),
        name="gdn",
    )(qkv3, qkv3, qkv3, gate3, abt3, alog8, dtb8, conv_w, conv_w, conv_w, norm_g.reshape(1, DV))


def _mixer_gdn(x2, B, S, g, w_in, conv_w, a_log, dt_bias, norm_g):
    w_ab = w_in[:, 4 * DC:].reshape(D_MODEL, 2, 2, HC).transpose(3, 1, 2, 0).reshape(HC, 4, D_MODEL)
    w_ab = jnp.concatenate([w_ab, jnp.zeros_like(w_ab)], axis=1).reshape(HC * SUBLANES, D_MODEL)
    qkv, gate, abt = _inproj_t(x2, g, w_in[:, :4 * DC].astype(BF16), w_ab.astype(BF16),
                               ((0, 3 * DC), (3 * DC, 4 * DC)), (F32, BF16))

    def per_head(p):
        p8 = jnp.concatenate([p.T.astype(F32), jnp.zeros((HC, SUBLANES - 2), F32)], axis=1)
        return jnp.broadcast_to(p8[:, :, None], (HC, SUBLANES, LANES))

    y = _gdn_core(qkv.reshape(B, S, 3 * DC), gate.reshape(B, S, DC), abt.reshape(HC, SUBLANES, B * S),
                  per_head(a_log), per_head(dt_bias), conv_w, norm_g)
    return y.reshape(B * S, DC)


def kernel(x, norm_mix_g, norm_mlp_g, mlp_w_up, mlp_w_down, norm_final_g, a_w_in, a_conv_w, a_conv_b, a_gate_w, a_gate_b, a_lambda, a_w_out, b_w_in, b_ln_g, b_ln_b, b_w_s, b_b_s, b_w_out, c_w_in, c_conv_w, c_a_log, c_dt_bias, c_norm_g, c_w_out):
    B, S, _ = x.shape
    depth = norm_mix_g.shape[0]
    x2 = x.reshape(B * S, D_MODEL)
    for i in range(depth):
        kind, j = i % N_MIXERS, i // N_MIXERS
        g = norm_mix_g[i]
        y, w_out = None, None
        if kind == 0:
            y = _mixer_rglru(x2, B, S, g, a_w_in[j], a_conv_w[j], a_conv_b[j], a_gate_w[j], a_gate_b[j], a_lambda[j])
            w_out = a_w_out[j].astype(BF16)
        elif kind == 1:
            x2 = _sgu(x2, g, b_w_in[j], b_ln_g[j], b_ln_b[j], b_w_s[j], b_b_s[j], b_w_out[j])
        else:
            y = _mixer_gdn(x2, B, S, g, c_w_in[j], c_conv_w[j], c_a_log[j], c_dt_bias[j], c_norm_g[j])
            w_out = c_w_out[j].astype(BF16)
        x2 = _mlp(x2, norm_mlp_g[i], mlp_w_up[i].astype(BF16), mlp_w_down[i].astype(BF16),
                  norm_final_g if i == depth - 1 else None, y, w_out)
    return x2.reshape(B, S, D_MODEL)
```
